```python
import math
import jax, jax.numpy as jnp
from jax import lax
import numpy as np

D_MODEL = 1024
BATCH = 16
SEQ = 2048
DEPTH = 4

N_POOL_GROUPS = 4
POOL_GROUP_DIM = 128
POOL_WIDTH = N_POOL_GROUPS * POOL_GROUP_DIM
POOL_WINDOWS = (2, 4, 8, 16)

ATT_HEADS = 4
ATT_QK_DIM = 64
ATT_V_DIM = 2 * ATT_QK_DIM
ATT_QK_WIDTH = ATT_HEADS * 2 * ATT_QK_DIM
ATT_WIDTH = ATT_HEADS * ATT_V_DIM
Q_BLOCK = 128

REC_HEADS = 4
REC_K_DIM = 128
REC_V_DIM = 128
REC_K_WIDTH = REC_HEADS * REC_K_DIM
REC_WIDTH = REC_HEADS * REC_V_DIM
REC_CHUNK = 64

N_BRANCHES = 3
FFN_HIDDEN = ((math.ceil(8 * D_MODEL / 3) + 255) // 256) * 256
NORM_EPS = 1e-6

IN_SIZES = (
    POOL_WIDTH,
    ATT_QK_WIDTH, ATT_QK_WIDTH, ATT_WIDTH,
    REC_K_WIDTH, REC_K_WIDTH, REC_K_WIDTH,
    REC_WIDTH, REC_WIDTH,
    N_BRANCHES * D_MODEL,
)
IN_COLS = (POOL_WIDTH + 2 * ATT_QK_WIDTH + ATT_WIDTH + 3 * REC_K_WIDTH
           + 2 * REC_WIDTH + N_BRANCHES * D_MODEL)

kernel_name = "hybrid_pool_diffattn_hgrn2_encoder"


def rms_norm(x, g):
    xf = x.astype(jnp.float32)
    y = xf * lax.rsqrt(jnp.mean(xf * xf, axis=-1, keepdims=True) + NORM_EPS)
    return (y * g).astype(x.dtype)


def split_cols(proj):
    out, off = [], 0
    for n in IN_SIZES:
        out.append(proj[..., off:off + n])
        off += n
    return out


def pool_mixer(u, pool_w, pool_scale):
    B, S, _ = u.shape
    uf = u.astype(jnp.float32).reshape(B, S, N_POOL_GROUPS, POOL_GROUP_DIM)
    cs = jnp.concatenate([jnp.zeros((B, 1, N_POOL_GROUPS, POOL_GROUP_DIM), jnp.float32),
                          jnp.cumsum(uf, axis=1)], axis=1)
    t = jnp.arange(S)
    outs = []
    for g, w in enumerate(POOL_WINDOWS):
        lo = jnp.clip(t - w // 2, 0, S - 1)
        hi = jnp.clip(t + w // 2 - 1, 0, S - 1)
        cnt = (hi - lo + 1).astype(jnp.float32)
        mean = (cs[:, hi + 1, g] - cs[:, lo, g]) / cnt[None, :, None]
        outs.append(mean - uf[:, :, g])
    d = jnp.stack(outs, axis=2).astype(u.dtype)
    y = jnp.einsum('bsgc,gcd->bsgd', d, pool_w).reshape(B, S, POOL_WIDTH)
    return y * pool_scale


def diff_attention(q, k, v, lam, gain, lambda_init):
    B, S, _ = q.shape
    q = q.reshape(B, S, ATT_HEADS, 2, ATT_QK_DIM)
    k = k.reshape(B, S, ATT_HEADS, 2, ATT_QK_DIM)
    v = v.reshape(B, S, ATT_HEADS, ATT_V_DIM)
    scale = ATT_QK_DIM ** -0.5
    slopes = jnp.exp2(-8.0 * jnp.arange(1, ATT_HEADS + 1, dtype=jnp.float32) / ATT_HEADS)
    pos = jnp.arange(S, dtype=jnp.float32)
    nb = S // Q_BLOCK
    qb = q.reshape(B, nb, Q_BLOCK, ATT_HEADS, 2, ATT_QK_DIM).swapaxes(0, 1)
    pb = pos.reshape(nb, Q_BLOCK)

    def block(args):
        qblk, qpos = args
        s = jnp.einsum('bqhcd,bkhcd->bhcqk', qblk, k,
                       preferred_element_type=jnp.float32) * scale
        dist = jnp.abs(qpos[:, None] - pos[None, :])
        s = s - slopes[:, None, None, None] * dist
        p = jax.nn.softmax(s, axis=-1)
        a = (p[:, :, 0] - lam * p[:, :, 1]).astype(v.dtype)
        return jnp.einsum('bhqk,bkhe->bqhe', a, v)

    o = lax.map(block, (qb, pb))
    o = o.swapaxes(0, 1).reshape(B, S, ATT_HEADS, ATT_V_DIM)
    o = rms_norm(o, gain) * (1.0 - lambda_init)
    return o.reshape(B, S, ATT_WIDTH)


def gla_direction(q, k, v, logf):
    B, S, H, DK = q.shape
    DV = v.shape[-1]
    L = REC_CHUNK
    nc = S // L

    def chunks(t):
        return t.astype(jnp.float32).reshape(B, nc, L, H, t.shape[-1]).transpose(1, 0, 3, 2, 4)

    tril = jnp.tril(jnp.ones((L, L), dtype=bool))[:, :, None]

    def step(state, inp):
        qc, kc, vc, gc = inp
        b = jnp.cumsum(gc, axis=2)
        b_last = b[:, :, -1:, :]
        o_inter = jnp.einsum('bhtk,bhkv->bhtv', qc * jnp.exp(b), state)
        rel = b[:, :, :, None, :] - b[:, :, None, :, :]
        decay = jnp.exp(jnp.where(tril, rel, -jnp.inf))
        scores = jnp.einsum('bhtk,bhsk,bhtsk->bhts', qc, kc, decay)
        o_intra = jnp.einsum('bhts,bhsv->bhtv', scores, vc)
        state = (jnp.exp(b_last).swapaxes(-1, -2) * state
                 + jnp.einsum('bhsk,bhsv->bhkv', kc * jnp.exp(b_last - b), vc))
        return state, o_inter + o_intra

    s0 = jnp.zeros((B, H, DK, DV), jnp.float32)
    _, o = lax.scan(step, s0, (chunks(q), chunks(k), chunks(v), chunks(logf)))
    return o.transpose(1, 0, 3, 2, 4).reshape(B, S, H, DV)


def hgrn2_mixer(q, f_fwd, f_bwd, i, g, lb_fwd, lb_bwd, gain):
    B, S, _ = q.shape
    heads = lambda t, d: t.reshape(B, S, REC_HEADS, d)
    qh = heads(q, REC_K_DIM).astype(jnp.float32) * (REC_K_DIM ** -0.5)
    ih = heads(i, REC_V_DIM)

    def gates(z, lb):
        lb = lb.astype(jnp.float32).reshape(REC_HEADS, REC_K_DIM)
        logf = jnp.logaddexp(jnp.log(lb), jnp.log1p(-lb)
                             + jax.nn.log_sigmoid(heads(z, REC_K_DIM).astype(jnp.float32)))
        return -jnp.expm1(logf), logf

    kf, lf = gates(f_fwd, lb_fwd)
    kb, lbk = gates(f_bwd, lb_bwd)
    o_f = gla_direction(qh, kf, ih, lf)
    flip = lambda t: jnp.flip(t, axis=1)
    o_b = flip(gla_direction(flip(qh), flip(kb), flip(ih), flip(lbk)))
    o = rms_norm((o_f + o_b).astype(q.dtype), gain).reshape(B, S, REC_WIDTH)
    return o * jax.nn.silu(g)


def setup_inputs(seed: int = 0) -> dict:
    key = jax.random.key(seed)
    ks = jax.random.split(key, 21)
    f32 = jnp.float32
    nrm = lambda k, shape: jax.random.normal(k, shape, f32)
    return {
        "x": nrm(ks[0], (BATCH, SEQ, D_MODEL)),
        "norm1_g": 1.0 + 0.02 * nrm(ks[1], (DEPTH, D_MODEL)),
        "w_in": nrm(ks[2], (DEPTH, D_MODEL, IN_COLS)) * D_MODEL ** -0.5,
        "pool_w": nrm(ks[3], (DEPTH, N_POOL_GROUPS, POOL_GROUP_DIM, POOL_GROUP_DIM)) * POOL_GROUP_DIM ** -0.5,
        "pool_scale": 1.0 + 0.1 * nrm(ks[4], (DEPTH, POOL_WIDTH)),
        "lam_q1": 0.1 * nrm(ks[5], (DEPTH, ATT_QK_DIM)),
        "lam_k1": 0.1 * nrm(ks[6], (DEPTH, ATT_QK_DIM)),
        "lam_q2": 0.1 * nrm(ks[7], (DEPTH, ATT_QK_DIM)),
        "lam_k2": 0.1 * nrm(ks[8], (DEPTH, ATT_QK_DIM)),
        "diff_norm_g": 1.0 + 0.02 * nrm(ks[9], (DEPTH, ATT_V_DIM)),
        "hgrn_lb": 0.5 * nrm(ks[10], (2, DEPTH, REC_K_WIDTH)),
        "hgrn_norm_g": 1.0 + 0.02 * nrm(ks[11], (DEPTH, REC_V_DIM)),
        "w_up_pool": nrm(ks[12], (DEPTH, POOL_WIDTH, D_MODEL)) * POOL_WIDTH ** -0.5,
        "w_up_attn": nrm(ks[13], (DEPTH, ATT_WIDTH, D_MODEL)) * ATT_WIDTH ** -0.5,
        "w_up_rec": nrm(ks[14], (DEPTH, REC_WIDTH, D_MODEL)) * REC_WIDTH ** -0.5,
        "w_out": nrm(ks[15], (DEPTH, D_MODEL, D_MODEL)) * D_MODEL ** -0.5,
        "norm2_g": 1.0 + 0.02 * nrm(ks[16], (DEPTH, D_MODEL)),
        "w_ffn_in": nrm(ks[17], (DEPTH, D_MODEL, 2 * FFN_HIDDEN)) * D_MODEL ** -0.5,
        "w_ffn_out": nrm(ks[18], (DEPTH, FFN_HIDDEN, D_MODEL)) * FFN_HIDDEN ** -0.5,
        "final_norm_g": 1.0 + 0.02 * nrm(ks[19], (D_MODEL,)),
    }


def reference(x, norm1_g, w_in, pool_w, pool_scale, lam_q1, lam_k1, lam_q2, lam_k2,
              diff_norm_g, hgrn_lb, hgrn_norm_g, w_up_pool, w_up_attn, w_up_rec, w_out,
              norm2_g, w_ffn_in, w_ffn_out, final_norm_g):
    B, S, _ = x.shape
    lb_all = jnp.cumsum(jax.nn.softmax(hgrn_lb.astype(jnp.float32), axis=1), axis=1)
    lb_all = lb_all - lb_all[:, :1]
    for l in range(DEPTH):
        h = rms_norm(x, norm1_g[l])
        proj = h @ w_in[l]
        u_pool, aq, ak, av, rq, rf, rb, ri, rg, gate_pre = split_cols(proj)

        y_pool = pool_mixer(u_pool, pool_w[l], pool_scale[l])

        lambda_init = 0.8 - 0.6 * math.exp(-0.3 * l)
        lam = (jnp.exp(jnp.sum(lam_q1[l].astype(jnp.float32) * lam_k1[l].astype(jnp.float32)))
               - jnp.exp(jnp.sum(lam_q2[l].astype(jnp.float32) * lam_k2[l].astype(jnp.float32)))
               + lambda_init)
        y_attn = diff_attention(aq, ak, av, lam, diff_norm_g[l], lambda_init)

        y_rec = hgrn2_mixer(rq, rf, rb, ri, rg, lb_all[0, l], lb_all[1, l], hgrn_norm_g[l])

        gates = jax.nn.sigmoid(gate_pre.reshape(B, S, N_BRANCHES, D_MODEL))
        merged = (gates[:, :, 0] * (y_pool @ w_up_pool[l])
                  + gates[:, :, 1] * (y_attn @ w_up_attn[l])
                  + gates[:, :, 2] * (y_rec @ w_up_rec[l]))
        x = x + merged @ w_out[l]

        h2 = rms_norm(x, norm2_g[l])
        gu = h2 @ w_ffn_in[l]
        x = x + (jax.nn.silu(gu[..., :FFN_HIDDEN]) * gu[..., FFN_HIDDEN:]) @ w_ffn_out[l]
    return rms_norm(x, final_norm_g)
```

```python
import functools
import math

import jax
import jax.numpy as jnp
from jax import lax
from jax.experimental import pallas as pl
from jax.experimental.pallas import tpu as pltpu

F32 = jnp.float32
BF16 = jnp.bfloat16

D_MODEL = 1024
DEPTH = 4
N_POOL_GROUPS = 4
POOL_GROUP_DIM = 128
POOL_WIDTH = N_POOL_GROUPS * POOL_GROUP_DIM
POOL_WINDOWS = (2, 4, 8, 16)
ATT_HEADS = 4
ATT_QK_DIM = 64
ATT_V_DIM = 2 * ATT_QK_DIM
ATT_WIDTH = ATT_HEADS * ATT_V_DIM
REC_HEADS = 4
REC_K_DIM = 128
REC_V_DIM = 128
REC_WIDTH = REC_HEADS * REC_V_DIM
N_BRANCHES = 3
FFN_HIDDEN = ((math.ceil(8 * D_MODEL / 3) + 255) // 256) * 256
NORM_EPS = 1e-6

SUBLANES = 8
LANES = 128
VMEM_LIMIT_BYTES = 56 * 1024 * 1024

IN_SECTIONS = (
    ("pool", POOL_WIDTH, BF16, 1.0),
    ("aq", ATT_WIDTH, BF16, ATT_QK_DIM ** -0.5),
    ("ak", ATT_WIDTH, BF16, 1.0),
    ("av", ATT_WIDTH, BF16, 1.0),
    ("rq", REC_WIDTH, BF16, REC_K_DIM ** -0.5),
    ("rf", REC_WIDTH, F32, 1.0),
    ("rb", REC_WIDTH, F32, 1.0),
    ("ri", REC_WIDTH, BF16, 1.0),
    ("rg", REC_WIDTH, BF16, 1.0),
    ("gate", N_BRANCHES * D_MODEL, BF16, 1.0),
)
IN_COLS = sum(s[1] for s in IN_SECTIONS)

CHUNK = 64
SUB = SUBLANES
NEG_BIG = -1e30


def _params(*sem):
    return pltpu.CompilerParams(dimension_semantics=sem, vmem_limit_bytes=VMEM_LIMIT_BYTES)


def _resident(shape):
    return pl.BlockSpec(shape, lambda *_: (0,) * len(shape), pipeline_mode=pl.Buffered(1))


def _rms(x, g):
    return x * lax.rsqrt(jnp.mean(x * x, axis=-1, keepdims=True) + NORM_EPS) * g


def _inproj_body(x_ref, g_ref, w_ref, *out_refs):
    h = _rms(x_ref[...], g_ref[...]).astype(BF16)
    off = 0
    for o_ref, (_, width, _, scale) in zip(out_refs, IN_SECTIONS):
        for c in range(0, width, 512):
            y = jnp.dot(h, w_ref[:, off + c:off + c + 512], preferred_element_type=F32)
            if scale != 1.0:
                y = y * scale
            o_ref[:, c:c + 512] = y.astype(o_ref.dtype)
        off += width


def _inproj(x2, g, w, tm):
    T = x2.shape[0]
    return pl.pallas_call(
        _inproj_body,
        grid=(T // tm,),
        in_specs=[
            pl.BlockSpec((tm, D_MODEL), lambda i: (i, 0)),
            _resident((1, D_MODEL)),
            _resident((D_MODEL, IN_COLS)),
        ],
        out_specs=[pl.BlockSpec((tm, wd), lambda i: (i, 0)) for _, wd, _, _ in IN_SECTIONS],
        out_shape=[jax.ShapeDtypeStruct((T, wd), dt) for _, wd, dt, _ in IN_SECTIONS],
        compiler_params=_params("arbitrary"),
        name="inproj",
    )(x2, g, w)


def _pool_body(u_ref, w_ref, sc_ref, o_ref, *, S):
    t = lax.broadcasted_iota(jnp.int32, (S, POOL_GROUP_DIM), 0)
    for g, win in enumerate(POOL_WINDOWS):
        cols = slice(g * POOL_GROUP_DIM, (g + 1) * POOL_GROUP_DIM)
        u = u_ref[:, cols].astype(F32)
        tot = u
        for o in range(-(win // 2), win // 2):
            if o == 0:
                continue
            shifted = pltpu.roll(u, (-o) % S, axis=0)
            valid = (t + o >= 0) & (t + o < S)
            tot = tot + jnp.where(valid, shifted, 0.0)
        lo = jnp.maximum(t - win // 2, 0)
        hi = jnp.minimum(t + win // 2 - 1, S - 1)
        cnt = (hi - lo + 1).astype(F32)
        d = (tot / cnt - u).astype(BF16)
        y = jnp.dot(d, w_ref[g], preferred_element_type=F32) * sc_ref[:, cols]
        o_ref[:, cols] = y.astype(o_ref.dtype)


def _pool(u, w, sc, B, S):
    return pl.pallas_call(
        functools.partial(_pool_body, S=S),
        grid=(B,),
        in_specs=[
            pl.BlockSpec((S, POOL_WIDTH), lambda b: (b, 0)),
            _resident((N_POOL_GROUPS, POOL_GROUP_DIM, POOL_GROUP_DIM)),
            _resident((1, POOL_WIDTH)),
        ],
        out_specs=pl.BlockSpec((S, POOL_WIDTH), lambda b: (b, 0)),
        out_shape=jax.ShapeDtypeStruct((B * S, POOL_WIDTH), BF16),
        compiler_params=_params("arbitrary"),
        name="pool",
    )(u, w, sc)


def _attn_body(q_ref, k_ref, v_ref, lq1_ref, lk1_ref, lq2_ref, lk2_ref, gain_ref, o_ref,
               *, S, qb, lambda_init):
    qi = pl.program_id(1)
    lam = (jnp.exp(jnp.sum(lq1_ref[...] * lk1_ref[...], axis=-1, keepdims=True))
           - jnp.exp(jnp.sum(lq2_ref[...] * lk2_ref[...], axis=-1, keepdims=True))
           + lambda_init)
    qpos = qi * qb + lax.broadcasted_iota(jnp.int32, (qb, S), 0)
    kpos = lax.broadcasted_iota(jnp.int32, (qb, S), 1)
    dist = jnp.abs(qpos - kpos).astype(F32)
    lane = lax.broadcasted_iota(jnp.int32, (qb, ATT_V_DIM), 1)
    nt = (((1,), (1,)), ((), ()))
    for h in range(ATT_HEADS):
        cols = slice(h * ATT_V_DIM, (h + 1) * ATT_V_DIM)
        qh = q_ref[:, cols]
        kh = k_ref[:, cols]
        bias = dist * (2.0 ** (-8.0 * (h + 1) / ATT_HEADS))
        probs = []
        for c in range(2):
            keep = (lane < ATT_QK_DIM) if c == 0 else (lane >= ATT_QK_DIM)
            qc = jnp.where(keep, qh, jnp.zeros_like(qh))
            s = lax.dot_general(qc, kh, nt, preferred_element_type=F32) - bias
            p = jnp.exp(s - jnp.max(s, axis=-1, keepdims=True))
            probs.append(p * (1.0 / jnp.sum(p, axis=-1, keepdims=True)))
        a = (probs[0] - lam * probs[1]).astype(BF16)
        o = jnp.dot(a, v_ref[:, cols], preferred_element_type=F32)
        o = _rms(o, gain_ref[...]) * (1.0 - lambda_init)
        o_ref[:, cols] = o.astype(o_ref.dtype)


def _attn(q, k, v, lq1, lk1, lq2, lk2, gain, B, S, qb, lambda_init):
    small = _resident((1, ATT_QK_DIM))
    return pl.pallas_call(
        functools.partial(_attn_body, S=S, qb=qb, lambda_init=lambda_init),
        grid=(B, S // qb),
        in_specs=[
            pl.BlockSpec((qb, ATT_WIDTH), lambda b, i: (b * (S // qb) + i, 0)),
            pl.BlockSpec((S, ATT_WIDTH), lambda b, i: (b, 0)),
            pl.BlockSpec((S, ATT_WIDTH), lambda b, i: (b, 0)),
            small, small, small, small,
            _resident((1, ATT_V_DIM)),
        ],
        out_specs=pl.BlockSpec((qb, ATT_WIDTH), lambda b, i: (b * (S // qb) + i, 0)),
        out_shape=jax.ShapeDtypeStruct((B * S, ATT_WIDTH), BF16),
        compiler_params=_params("arbitrary", "arbitrary"),
        name="diffattn",
    )(q, k, v, lq1, lk1, lq2, lk2, gain)


def _forget_gates(z, lb):
    a = jnp.log(lb)
    l1m = jnp.log1p(-lb)
    log_sig = jnp.minimum(z, 0.0) - jnp.log1p(jnp.exp(-jnp.abs(z)))
    c = l1m + log_sig
    logf = jnp.maximum(a, c) + jnp.log1p(jnp.exp(-jnp.abs(a - c)))
    return logf, jnp.exp(l1m + (log_sig - z))


def _chunk_cumsum(x, row, rev):
    sh = 1
    while sh < CHUNK:
        if rev:
            x = x + jnp.where(row < CHUNK - sh, pltpu.roll(x, CHUNK - sh, axis=0), 0.0)
        else:
            x = x + jnp.where(row >= sh, pltpu.roll(x, sh, axis=0), 0.0)
        sh *= 2
    return x


def _same_block(r, c, size):
    return ((r ^ c) & ~(size - 1)) == 0


def _rows_bcast(ref, rows, n):
    parts = []
    for r in rows:
        if r is None:
            parts.append(jnp.zeros((n, LANES), F32))
        else:
            parts.append(jnp.broadcast_to(ref[r:r + 1, :], (n, LANES)))
    return parts[0] if len(parts) == 1 else jnp.concatenate(parts, axis=0)


def _hgrn_chunk(r0, rev, q_ref, z_ref, v_ref, lb, esel_ref, st_ref, b_ref, k_ref, acc_ref, row, rt, ct):
    nsub = CHUNK // SUB
    nt = (((1,), (1,)), ((), ()))
    logf, kk = _forget_gates(z_ref[pl.ds(r0, CHUNK), :], lb)
    b = _chunk_cumsum(logf, row, rev)
    b_ref[...] = b
    k_ref[...] = kk
    q = q_ref[pl.ds(r0, CHUNK), :].astype(F32)
    v = v_ref[pl.ds(r0, CHUNK), :]

    if rev:
        ends = [SUB * i for i in range(nsub)]
        starts = [SUB * (i + 1) if i + 1 < nsub else None for i in range(nsub)]
        last = 0
    else:
        ends = [SUB * i + SUB - 1 for i in range(nsub)]
        starts = [SUB * i - 1 if i > 0 else None for i in range(nsub)]
        last = CHUNK - 1
    be = _rows_bcast(b_ref, ends, SUB)
    bs = _rows_bcast(b_ref, starts, SUB)
    b_last = b_ref[last:last + 1, :]

    q_hat = q * jnp.exp(b - bs)
    k_hat = kk * jnp.exp(be - b)

    st = st_ref[...]
    o = lax.dot_general((q * jnp.exp(b)).astype(BF16), st.astype(BF16), nt, preferred_element_type=F32)

    rsub = row & (SUB - 1)
    pieces = []
    for sp in range(SUB):
        rows = [SUB * i + sp for i in range(nsub)]
        valid = (rsub <= sp) if rev else (rsub >= sp)
        e = jnp.exp(jnp.where(valid, b - _rows_bcast(b_ref, rows, SUB), NEG_BIG))
        pieces.append((q * _rows_bcast(k_ref, rows, SUB) * e).astype(BF16))
    a = jnp.dot(jnp.concatenate(pieces, axis=1), esel_ref[...], preferred_element_type=F32)
    a = jnp.where(_same_block(rt, ct, SUB), a, 0.0)

    m = 2 * SUB
    while m <= CHUNK:
        half = m // 2
        mids = [m * j + (half if rev else half - 1) for j in range(CHUNK // m)]
        bmid = _rows_bcast(b_ref, mids, m)
        in_blk = row & (m - 1)
        late = (in_blk < half) if rev else (in_blk >= half)
        q_m = q_hat * jnp.exp(jnp.where(late, bs - bmid, NEG_BIG))
        k_m = k_hat * jnp.exp(jnp.where(late, NEG_BIG, bmid - be))
        a_m = lax.dot_general(q_m.astype(BF16), k_m.astype(BF16), nt, preferred_element_type=F32)
        a = a + jnp.where(_same_block(rt, ct, m), a_m, 0.0)
        m *= 2

    o = o + jnp.dot(a.astype(BF16), v, preferred_element_type=F32)
    acc_ref[pl.ds(r0, CHUNK), :] += o

    k_up = (kk * jnp.exp(b_last - b)).astype(BF16)
    tn = (((0,), (0,)), ((), ()))
    st_ref[...] = st * jnp.exp(b_last) + lax.dot_general(v, k_up, tn, preferred_element_type=F32)


def _hgrn_body(q_ref, zf_ref, zb_ref, v_ref, g_ref, lb_ref, gain_ref, esel_ref, o_ref,
               stf_ref, stb_ref, bf_ref, bb_ref, kf_ref, kb_ref, acc_ref, *, S):
    nchunks = S // CHUNK
    stf_ref[...] = jnp.zeros_like(stf_ref)
    stb_ref[...] = jnp.zeros_like(stb_ref)
    acc_ref[...] = jnp.zeros_like(acc_ref)
    row = lax.broadcasted_iota(jnp.int32, (CHUNK, LANES), 0)
    rt = lax.broadcasted_iota(jnp.int32, (CHUNK, CHUNK), 0)
    ct = lax.broadcasted_iota(jnp.int32, (CHUNK, CHUNK), 1)
    lb_f = lb_ref[0:1, :]
    lb_b = lb_ref[1:2, :]

    def step(c, carry):
        rf = pl.multiple_of(c * CHUNK, CHUNK)
        rb = pl.multiple_of((nchunks - 1 - c) * CHUNK, CHUNK)
        _hgrn_chunk(rf, False, q_ref, zf_ref, v_ref, lb_f, esel_ref, stf_ref, bf_ref, kf_ref, acc_ref,
                    row, rt, ct)
        _hgrn_chunk(rb, True, q_ref, zb_ref, v_ref, lb_b, esel_ref, stb_ref, bb_ref, kb_ref, acc_ref,
                    row, rt, ct)
        return carry

    lax.fori_loop(0, nchunks, step, 0)
    o = _rms(acc_ref[...], gain_ref[...])
    g = g_ref[...].astype(F32)
    o_ref[...] = (o * (g * jax.nn.sigmoid(g))).astype(o_ref.dtype)


def _hgrn(q, zf, zb, v, g, lb, gain, esel, B, S):
    blk = lambda: pl.BlockSpec((S, REC_K_DIM), lambda b, h: (b, h))
    return pl.pallas_call(
        functools.partial(_hgrn_body, S=S),
        grid=(B, REC_HEADS),
        in_specs=[
            blk(), blk(), blk(), blk(), blk(),
            pl.BlockSpec((2, REC_K_DIM), lambda b, h: (0, h)),
            _resident((1, REC_V_DIM)),
            _resident((SUB * REC_K_DIM, CHUNK)),
        ],
        out_specs=blk(),
        out_shape=jax.ShapeDtypeStruct((B * S, REC_WIDTH), BF16),
        scratch_shapes=[
            pltpu.VMEM((REC_V_DIM, REC_K_DIM), F32),
            pltpu.VMEM((REC_V_DIM, REC_K_DIM), F32),
            pltpu.VMEM((CHUNK, REC_K_DIM), F32),
            pltpu.VMEM((CHUNK, REC_K_DIM), F32),
            pltpu.VMEM((CHUNK, REC_K_DIM), F32),
            pltpu.VMEM((CHUNK, REC_K_DIM), F32),
            pltpu.VMEM((S, REC_V_DIM), F32),
        ],
        compiler_params=_params("arbitrary", "arbitrary"),
        name="hgrn2",
    )(q, zf, zb, v, g, lb, gain, esel)


FFN_TILE = 256


def _merge_ffn_body(x_ref, yp_ref, ya_ref, yr_ref, gate_ref, wup_ref, wua_ref, wur_ref, wo_ref,
                    n2_ref, wfi_ref, wfo_ref, fg_ref, o_ref, *, final):
    merged = None
    for k, (y_ref, w_ref) in enumerate(((yp_ref, wup_ref), (ya_ref, wua_ref), (yr_ref, wur_ref))):
        gate = jax.nn.sigmoid(gate_ref[:, k * D_MODEL:(k + 1) * D_MODEL].astype(F32))
        term = gate * jnp.dot(y_ref[...], w_ref[...], preferred_element_type=F32)
        merged = term if merged is None else merged + term
    x1 = x_ref[...] + jnp.dot(merged.astype(BF16), wo_ref[...], preferred_element_type=F32)
    h2 = _rms(x1, n2_ref[...]).astype(BF16)
    acc = x1
    for c in range(0, FFN_HIDDEN, FFN_TILE):
        gp = jnp.dot(h2, wfi_ref[:, c:c + FFN_TILE], preferred_element_type=F32)
        up = jnp.dot(h2, wfi_ref[:, FFN_HIDDEN + c:FFN_HIDDEN + c + FFN_TILE], preferred_element_type=F32)
        act = (gp * jax.nn.sigmoid(gp) * up).astype(BF16)
        acc = acc + jnp.dot(act, wfo_ref[c:c + FFN_TILE, :], preferred_element_type=F32)
    if final:
        acc = _rms(acc, fg_ref[...])
    o_ref[...] = acc


def _merge_ffn(x2, yp, ya, yr, gate, wup, wua, wur, wo, n2, wfi, wfo, fg, tm, final):
    T = x2.shape[0]
    row = lambda wd: pl.BlockSpec((tm, wd), lambda i: (i, 0))
    return pl.pallas_call(
        functools.partial(_merge_ffn_body, final=final),
        grid=(T // tm,),
        in_specs=[
            row(D_MODEL), row(POOL_WIDTH), row(ATT_WIDTH), row(REC_WIDTH), row(N_BRANCHES * D_MODEL),
            _resident((POOL_WIDTH, D_MODEL)), _resident((ATT_WIDTH, D_MODEL)), _resident((REC_WIDTH, D_MODEL)),
            _resident((D_MODEL, D_MODEL)), _resident((1, D_MODEL)),
            _resident((D_MODEL, 2 * FFN_HIDDEN)), _resident((FFN_HIDDEN, D_MODEL)), _resident((1, D_MODEL)),
        ],
        out_specs=row(D_MODEL),
        out_shape=jax.ShapeDtypeStruct((T, D_MODEL), F32),
        compiler_params=_params("arbitrary"),
        name="merge_ffn",
    )(x2, yp, ya, yr, gate, wup, wua, wur, wo, n2, wfi, wfo, fg)


def _diag_select():
    j = lax.broadcasted_iota(jnp.int32, (SUB * REC_K_DIM, CHUNK), 0) // REC_K_DIM
    c = lax.broadcasted_iota(jnp.int32, (SUB * REC_K_DIM, CHUNK), 1)
    return (c % SUB == j).astype(BF16)


@jax.jit
def kernel(x, norm1_g, w_in, pool_w, pool_scale, lam_q1, lam_k1, lam_q2, lam_k2, diff_norm_g, hgrn_lb,
           hgrn_norm_g, w_up_pool, w_up_attn, w_up_rec, w_out, norm2_g, w_ffn_in, w_ffn_out, final_norm_g):
    B, S, D = x.shape
    assert D == D_MODEL and w_in.shape[-1] == IN_COLS and S % CHUNK == 0
    T = B * S
    tm = min(512, T)
    qb = min(256, S)
    x2 = x.reshape(T, D).astype(F32)

    lb_all = jnp.cumsum(jax.nn.softmax(hgrn_lb.astype(F32), axis=1), axis=1)
    lb_all = lb_all - lb_all[:, :1]
    esel = _diag_select()
    row = lambda a: a.reshape(1, -1).astype(F32)

    for l in range(DEPTH):
        lambda_init = 0.8 - 0.6 * math.exp(-0.3 * l)
        (u_pool, aq, ak, av, rq, rf, rb, ri, rg, gate) = _inproj(
            x2, row(norm1_g[l]), w_in[l].astype(BF16), tm)
        y_pool = _pool(u_pool, pool_w[l].astype(BF16), row(pool_scale[l]), B, S)
        y_attn = _attn(aq, ak, av, row(lam_q1[l]), row(lam_k1[l]), row(lam_q2[l]), row(lam_k2[l]),
                       row(diff_norm_g[l]), B, S, qb, lambda_init)
        y_rec = _hgrn(rq, rf, rb, ri, rg, lb_all[:, l], row(hgrn_norm_g[l]), esel, B, S)
        x2 = _merge_ffn(x2, y_pool, y_attn, y_rec, gate,
                        w_up_pool[l].astype(BF16), w_up_attn[l].astype(BF16), w_up_rec[l].astype(BF16),
                        w_out[l].astype(BF16), row(norm2_g[l]), w_ffn_in[l].astype(BF16),
                        w_ffn_out[l].astype(BF16), row(final_norm_g), tm, l == DEPTH - 1)
    return x2.reshape(B, S, D).astype(x.dtype)
```

```python
import functools
import math

import jax
import jax.numpy as jnp
from jax import lax
from jax.experimental import pallas as pl
from jax.experimental.pallas import tpu as pltpu

F32 = jnp.float32
BF16 = jnp.bfloat16

D_MODEL = 1024
DEPTH = 4
N_POOL_GROUPS = 4
POOL_GROUP_DIM = 128
POOL_WIDTH = N_POOL_GROUPS * POOL_GROUP_DIM
POOL_WINDOWS = (2, 4, 8, 16)
ATT_HEADS = 4
ATT_QK_DIM = 64
ATT_V_DIM = 2 * ATT_QK_DIM
ATT_WIDTH = ATT_HEADS * ATT_V_DIM
REC_HEADS = 4
REC_K_DIM = 128
REC_V_DIM = 128
REC_WIDTH = REC_HEADS * REC_V_DIM
N_BRANCHES = 3
FFN_HIDDEN = ((math.ceil(8 * D_MODEL / 3) + 255) // 256) * 256
NORM_EPS = 1e-6
LOG2E = math.log2(math.e)

SUBLANES = 8
LANES = 128
VMEM_LIMIT_BYTES = 56 * 1024 * 1024

IN_SECTIONS = (
    ("pool", POOL_WIDTH, BF16, 1.0),
    ("aq", ATT_WIDTH, BF16, ATT_QK_DIM ** -0.5 * LOG2E),
    ("ak", ATT_WIDTH, BF16, 1.0),
    ("av", ATT_WIDTH, BF16, 1.0),
    ("rq", REC_WIDTH, BF16, REC_K_DIM ** -0.5),
    ("rf", REC_WIDTH, F32, 1.0),
    ("rb", REC_WIDTH, F32, 1.0),
    ("ri", REC_WIDTH, BF16, 1.0),
    ("rg", REC_WIDTH, BF16, 1.0),
    ("gate", N_BRANCHES * D_MODEL, BF16, 1.0),
)
IN_COLS = sum(s[1] for s in IN_SECTIONS)

CHUNK = 128
SUB = SUBLANES
NSUB = CHUNK // SUB
LEVELS = tuple(2 * SUB * 2 ** i for i in range(int(math.log2(CHUNK // (2 * SUB))) + 1))
NFAC = 2 + 2 * len(LEVELS)
NEG_BIG = -1e30


def _params(*sem):
    return pltpu.CompilerParams(dimension_semantics=sem, vmem_limit_bytes=VMEM_LIMIT_BYTES)


def _resident(shape):
    return pl.BlockSpec(shape, lambda *_: (0,) * len(shape), pipeline_mode=pl.Buffered(1))


def _rms(x, g):
    return x * lax.rsqrt(jnp.mean(x * x, axis=-1, keepdims=True) + NORM_EPS) * g


def _inproj_body(x_ref, g_ref, w_ref, *out_refs):
    h = _rms(x_ref[...], g_ref[...]).astype(BF16)
    off = 0
    for o_ref, (name, width, _, scale) in zip(out_refs, IN_SECTIONS):
        for c in range(0, width, 512):
            y = jnp.dot(h, w_ref[:, off + c:off + c + 512], preferred_element_type=F32)
            if scale != 1.0:
                y = y * scale
            if name == "av":
                for hd in range(ATT_HEADS):
                    o_ref[:, 2 * hd * ATT_V_DIM:(2 * hd + 1) * ATT_V_DIM] = (
                        y[:, hd * ATT_V_DIM:(hd + 1) * ATT_V_DIM].astype(o_ref.dtype))
                    o_ref[:, (2 * hd + 1) * ATT_V_DIM:(2 * hd + 2) * ATT_V_DIM] = jnp.ones(
                        (y.shape[0], ATT_V_DIM), o_ref.dtype)
            else:
                o_ref[:, c:c + 512] = y.astype(o_ref.dtype)
        off += width


def _out_width(name, width):
    return 2 * width if name == "av" else width


def _inproj(x2, g, w, tm):
    T = x2.shape[0]
    return pl.pallas_call(
        _inproj_body,
        grid=(T // tm,),
        in_specs=[
            pl.BlockSpec((tm, D_MODEL), lambda i: (i, 0)),
            _resident((1, D_MODEL)),
            _resident((D_MODEL, IN_COLS)),
        ],
        out_specs=[pl.BlockSpec((tm, _out_width(nm, wd)), lambda i: (i, 0)) for nm, wd, _, _ in IN_SECTIONS],
        out_shape=[jax.ShapeDtypeStruct((T, _out_width(nm, wd)), dt) for nm, wd, dt, _ in IN_SECTIONS],
        compiler_params=_params("arbitrary"),
        name="inproj",
    )(x2, g, w)


def _pool_body(u_ref, w_ref, sc_ref, o_ref, *, S):
    t = lax.broadcasted_iota(jnp.int32, (S, POOL_GROUP_DIM), 0)
    for g, win in enumerate(POOL_WINDOWS):
        cols = slice(g * POOL_GROUP_DIM, (g + 1) * POOL_GROUP_DIM)
        u = u_ref[:, cols].astype(F32)
        tot = u
        for o in range(-(win // 2), win // 2):
            if o == 0:
                continue
            shifted = pltpu.roll(u, (-o) % S, axis=0)
            valid = (t + o >= 0) & (t + o < S)
            tot = tot + jnp.where(valid, shifted, 0.0)
        lo = jnp.maximum(t - win // 2, 0)
        hi = jnp.minimum(t + win // 2 - 1, S - 1)
        cnt = (hi - lo + 1).astype(F32)
        d = (tot / cnt - u).astype(BF16)
        y = jnp.dot(d, w_ref[g], preferred_element_type=F32) * sc_ref[:, cols]
        o_ref[:, cols] = y.astype(o_ref.dtype)


def _pool(u, w, sc, B, S):
    return pl.pallas_call(
        functools.partial(_pool_body, S=S),
        grid=(B,),
        in_specs=[
            pl.BlockSpec((S, POOL_WIDTH), lambda b: (b, 0)),
            _resident((N_POOL_GROUPS, POOL_GROUP_DIM, POOL_GROUP_DIM)),
            _resident((1, POOL_WIDTH)),
        ],
        out_specs=pl.BlockSpec((S, POOL_WIDTH), lambda b: (b, 0)),
        out_shape=jax.ShapeDtypeStruct((B * S, POOL_WIDTH), BF16),
        compiler_params=_params("arbitrary"),
        name="pool",
    )(u, w, sc)


def _attn_body(q_ref, k_ref, v_ref, lq1_ref, lk1_ref, lq2_ref, lk2_ref, gain_ref, o_ref,
               *, S, qb, lambda_init):
    qi = pl.program_id(1)
    lam = (jnp.exp(jnp.sum(lq1_ref[...] * lk1_ref[...], axis=-1, keepdims=True))
           - jnp.exp(jnp.sum(lq2_ref[...] * lk2_ref[...], axis=-1, keepdims=True))
           + lambda_init)
    qpos = qi * qb + lax.broadcasted_iota(jnp.int32, (qb, S), 0)
    kpos = lax.broadcasted_iota(jnp.int32, (qb, S), 1)
    dist = jnp.abs(qpos - kpos).astype(F32)
    lane = lax.broadcasted_iota(jnp.int32, (qb, ATT_V_DIM), 1)
    nt = (((1,), (1,)), ((), ()))

    def logits(h):
        cols = slice(h * ATT_V_DIM, (h + 1) * ATT_V_DIM)
        qh = q_ref[:, cols]
        zero = jnp.zeros_like(qh)
        qs = jnp.concatenate([jnp.where(lane < ATT_QK_DIM, qh, zero),
                              jnp.where(lane >= ATT_QK_DIM, qh, zero)], axis=0)
        return lax.dot_general(qs, k_ref[:, cols], nt, preferred_element_type=F32)

    s_next = logits(0)
    for h in range(ATT_HEADS):
        cols = slice(h * ATT_V_DIM, (h + 1) * ATT_V_DIM)
        s = s_next
        if h + 1 < ATT_HEADS:
            s_next = logits(h + 1)
        bias = dist * (2.0 ** (-8.0 * (h + 1) / ATT_HEADS) * LOG2E)
        v_ext = v_ref[:, 2 * h * ATT_V_DIM:(2 * h + 2) * ATT_V_DIM]
        sm = []
        for c in range(2):
            sc = s[c * qb:(c + 1) * qb] - bias
            p = jnp.exp2(sc - jnp.max(sc, axis=-1, keepdims=True)).astype(BF16)
            ol = jnp.dot(p, v_ext, preferred_element_type=F32)
            sm.append(ol[:, :ATT_V_DIM] / ol[:, ATT_V_DIM:])
        o = _rms(sm[0] - lam * sm[1], gain_ref[...]) * (1.0 - lambda_init)
        o_ref[:, cols] = o.astype(o_ref.dtype)


def _attn(q, k, v, lq1, lk1, lq2, lk2, gain, B, S, qb, lambda_init):
    small = _resident((1, ATT_QK_DIM))
    return pl.pallas_call(
        functools.partial(_attn_body, S=S, qb=qb, lambda_init=lambda_init),
        grid=(B, S // qb),
        in_specs=[
            pl.BlockSpec((qb, ATT_WIDTH), lambda b, i: (b * (S // qb) + i, 0)),
            pl.BlockSpec((S, ATT_WIDTH), lambda b, i: (b, 0)),
            pl.BlockSpec((S, 2 * ATT_WIDTH), lambda b, i: (b, 0)),
            small, small, small, small,
            _resident((1, ATT_V_DIM)),
        ],
        out_specs=pl.BlockSpec((qb, ATT_WIDTH), lambda b, i: (b * (S // qb) + i, 0)),
        out_shape=jax.ShapeDtypeStruct((B * S, ATT_WIDTH), BF16),
        compiler_params=_params("arbitrary", "arbitrary"),
        name="diffattn",
    )(q, k, v, lq1, lk1, lq2, lk2, gain)


def _log_gates(z, lb):
    a = jnp.log(lb)
    l1m = jnp.log1p(-lb)
    soft = jnp.log(1.0 + jnp.exp(-jnp.abs(z)))
    c = l1m + (jnp.minimum(z, 0.0) - soft)
    logf = jnp.maximum(a, c) + jnp.log(1.0 + jnp.exp(-jnp.abs(a - c)))
    logk = l1m - jnp.maximum(z, 0.0) - soft
    return logf * LOG2E, logk * LOG2E


def _same_block(r, c, size):
    return ((r ^ c) & ~(size - 1)) == 0


def _rows_bcast(ref, rows, n):
    parts = [jnp.broadcast_to(ref[r:r + 1, :], (n, LANES)) for r in rows]
    return parts[0] if len(parts) == 1 else jnp.concatenate(parts, axis=0)


def _sub_block_factors(b_ref, fac_ref, rev, sub):
    be = b_ref[pl.ds(0 if rev else SUB - 1, NSUB, stride=SUB), :]
    if rev:
        bs = jnp.where(sub == NSUB - 1, 0.0, pltpu.roll(be, NSUB - 1, axis=0))
        last = 0
    else:
        bs = jnp.where(sub == 0, 0.0, pltpu.roll(be, 1, axis=0))
        last = CHUNK - 1
    b_last = b_ref[last:last + 1, :]
    fac_ref[0:NSUB, :] = jnp.exp2(bs)
    fac_ref[NSUB:2 * NSUB, :] = jnp.exp2(b_last - be)
    for j, m in enumerate(LEVELS):
        per = m // SUB
        mids = [m * blk + (m // 2 if rev else m // 2 - 1) for blk in range(CHUNK // m)]
        bmid = jnp.broadcast_to(b_ref[mids[-1]:mids[-1] + 1, :], (NSUB, LANES))
        for blk in range(CHUNK // m - 2, -1, -1):
            row = jnp.broadcast_to(b_ref[mids[blk]:mids[blk] + 1, :], (NSUB, LANES))
            bmid = jnp.where(sub < (blk + 1) * per, row, bmid)
        pos = sub & (per - 1)
        late = (pos < per // 2) if rev else (pos >= per // 2)
        fac_ref[(2 + 2 * j) * NSUB:(3 + 2 * j) * NSUB, :] = jnp.exp2(jnp.where(late, bs - bmid, NEG_BIG))
        fac_ref[(3 + 2 * j) * NSUB:(4 + 2 * j) * NSUB, :] = jnp.exp2(jnp.where(late, NEG_BIG, bmid - be))
    return b_last


def _expand(fac_ref, f):
    return _rows_bcast(fac_ref, [f * NSUB + i for i in range(NSUB)], SUB)


_NT = (((1,), (1,)), ((), ()))
_TN = (((0,), (0,)), ((), ()))


def _chunk_decays(r0, d, z_ref, lb_ref, tri):
    logf, logk = _log_gates(z_ref[pl.ds(r0, CHUNK), :], lb_ref[d:d + 1, :])
    hi = logf.astype(BF16)
    lo = (logf - hi.astype(F32)).astype(BF16)
    cs = jnp.dot(tri, jnp.concatenate([hi, lo], axis=1), preferred_element_type=F32)
    b_all = cs[:, :REC_WIDTH] + cs[:, REC_WIDTH:]
    return b_all, b_all - logk


def _chunk_operands(r0, d, rev, b_all, c_all, q_ref, esel_ref, b_ref, c_ref, fac_ref, sub):
    starts = ([SUB * (i + 1) for i in range(NSUB - 1)] + [None]) if rev else ([None] + [SUB * i - 1 for i in range(1, NSUB)])
    ends = [SUB * i for i in range(NSUB)] if rev else [SUB * i + SUB - 1 for i in range(NSUB)]
    heads, stacked = [], []
    for h in range(REC_HEADS):
        cols = slice(h * REC_K_DIM, (h + 1) * REC_K_DIM)
        bh, ch, fh = b_ref.at[d, h], c_ref.at[d, h], fac_ref.at[d, h]
        b = b_all[:, cols]
        c = c_all[:, cols]
        bh[...] = b
        ch[...] = c
        b_last = _sub_block_factors(bh, fh, rev, sub)
        q = q_ref[pl.ds(r0, CHUNK), cols].astype(F32)
        be = _rows_bcast(bh, ends, SUB)
        bs_parts = [jnp.zeros((SUB, LANES), F32) if r is None else jnp.broadcast_to(bh[r:r + 1, :], (SUB, LANES))
                    for r in starts]
        bs = jnp.concatenate(bs_parts, axis=0)
        q_hat = q * jnp.exp2(b - bs)
        k_hat = jnp.exp2(be - c)
        pieces = []
        for sp in range(SUB):
            crow = _rows_bcast(ch, [SUB * i + sp for i in range(NSUB)], SUB)
            pieces.append((q * jnp.exp2(jnp.minimum(b - crow, 0.0))).astype(BF16))
        stacked.append(jnp.concatenate(pieces, axis=1))
        heads.append((b_last, q_hat, k_hat))
    a_diag = jnp.dot(jnp.concatenate(stacked, axis=0), esel_ref[...], preferred_element_type=F32)
    return heads, a_diag


def _chunk_state_and_scores(r0, d, heads, a_diag, v_ref, st_ref, fac_ref, dmask, lmasks):
    out = []
    for h, (b_last, q_hat, k_hat) in enumerate(heads):
        cols = slice(h * REC_K_DIM, (h + 1) * REC_K_DIM)
        fh = fac_ref.at[d, h]
        v = v_ref[pl.ds(r0, CHUNK), cols]
        st = st_ref[d, h]
        o = lax.dot_general((q_hat * _expand(fh, 0)).astype(BF16), st.astype(BF16), _NT,
                            preferred_element_type=F32)
        k_up = (k_hat * _expand(fh, 1)).astype(BF16)
        st_ref[d, h] = st * jnp.exp2(b_last) + lax.dot_general(v, k_up, _TN, preferred_element_type=F32)
        a = jnp.where(dmask, a_diag[h * CHUNK:(h + 1) * CHUNK], 0.0)
        for j in range(len(LEVELS)):
            q_m = (q_hat * _expand(fh, 2 + 2 * j)).astype(BF16)
            k_m = (k_hat * _expand(fh, 3 + 2 * j)).astype(BF16)
            a_m = lax.dot_general(q_m, k_m, _NT, preferred_element_type=F32)
            a = a + jnp.where(lmasks[j], a_m, 0.0)
        out.append((o, a.astype(BF16), v))
    return out


def _chunk_outputs(r0, parts, acc_ref):
    for h, (o, a, v) in enumerate(parts):
        cols = slice(h * REC_V_DIM, (h + 1) * REC_V_DIM)
        acc_ref[pl.ds(r0, CHUNK), cols] = o + jnp.dot(a, v, preferred_element_type=F32)


def _hgrn_body(q_ref, zf_ref, zb_ref, v_ref, g_ref, lb_ref, gain_ref, esel_ref, o_ref,
               st_ref, b_ref, c_ref, fac_ref, accf_ref, accb_ref, *, S):
    nchunks = S // CHUNK
    st_ref[...] = jnp.zeros_like(st_ref)
    sub = lax.broadcasted_iota(jnp.int32, (NSUB, LANES), 0)
    rt = lax.broadcasted_iota(jnp.int32, (CHUNK, CHUNK), 0)
    ct = lax.broadcasted_iota(jnp.int32, (CHUNK, CHUNK), 1)
    lmasks = [_same_block(rt, ct, m) for m in LEVELS]
    same_sub = _same_block(rt, ct, SUB)
    dmasks = (same_sub & (rt >= ct), same_sub & (rt <= ct))
    tris = (jnp.where(ct <= rt, 1.0, 0.0).astype(BF16), jnp.where(ct >= rt, 1.0, 0.0).astype(BF16))
    z_refs = (zf_ref, zb_ref)
    acc_refs = (accf_ref, accb_ref)

    def step(i, carry):
        r0 = (pl.multiple_of(i * CHUNK, CHUNK), pl.multiple_of((nchunks - 1 - i) * CHUNK, CHUNK))
        decays = [_chunk_decays(r0[d], d, z_refs[d], lb_ref, tris[d]) for d in range(2)]
        ops = [_chunk_operands(r0[d], d, d == 1, *decays[d], q_ref, esel_ref, b_ref, c_ref, fac_ref, sub)
               for d in range(2)]
        parts = [_chunk_state_and_scores(r0[d], d, *ops[d], v_ref, st_ref, fac_ref, dmasks[d], lmasks)
                 for d in range(2)]
        for d in range(2):
            _chunk_outputs(r0[d], parts[d], acc_refs[d])
        return carry

    lax.fori_loop(0, nchunks, step, 0)
    for h in range(REC_HEADS):
        cols = slice(h * REC_V_DIM, (h + 1) * REC_V_DIM)
        o = _rms(accf_ref[:, cols] + accb_ref[:, cols], gain_ref[...])
        g = g_ref[:, cols].astype(F32)
        o_ref[:, cols] = (o * (g * jax.nn.sigmoid(g))).astype(o_ref.dtype)


def _hgrn(q, zf, zb, v, g, lb, gain, esel, B, S):
    blk = lambda: pl.BlockSpec((S, REC_WIDTH), lambda b: (b, 0))
    return pl.pallas_call(
        functools.partial(_hgrn_body, S=S),
        grid=(B,),
        in_specs=[
            blk(), blk(), blk(), blk(), blk(),
            _resident((2, REC_WIDTH)),
            _resident((1, REC_V_DIM)),
            _resident((SUB * REC_K_DIM, CHUNK)),
        ],
        out_specs=blk(),
        out_shape=jax.ShapeDtypeStruct((B * S, REC_WIDTH), BF16),
        scratch_shapes=[
            pltpu.VMEM((2, REC_HEADS, REC_V_DIM, REC_K_DIM), F32),
            pltpu.VMEM((2, REC_HEADS, CHUNK, REC_K_DIM), F32),
            pltpu.VMEM((2, REC_HEADS, CHUNK, REC_K_DIM), F32),
            pltpu.VMEM((2, REC_HEADS, NFAC * NSUB, REC_K_DIM), F32),
            pltpu.VMEM((S, REC_WIDTH), F32),
            pltpu.VMEM((S, REC_WIDTH), F32),
        ],
        compiler_params=_params("arbitrary"),
        name="hgrn2",
    )(q, zf, zb, v, g, lb, gain, esel)


FFN_TILE = 256


def _merge_ffn_body(x_ref, yp_ref, ya_ref, yr_ref, gate_ref, wup_ref, wua_ref, wur_ref, wo_ref,
                    n2_ref, wfi_ref, wfo_ref, fg_ref, o_ref, *, final):
    merged = None
    for k, (y_ref, w_ref) in enumerate(((yp_ref, wup_ref), (ya_ref, wua_ref), (yr_ref, wur_ref))):
        gate = jax.nn.sigmoid(gate_ref[:, k * D_MODEL:(k + 1) * D_MODEL].astype(F32))
        term = gate * jnp.dot(y_ref[...], w_ref[...], preferred_element_type=F32)
        merged = term if merged is None else merged + term
    x1 = x_ref[...] + jnp.dot(merged.astype(BF16), wo_ref[...], preferred_element_type=F32)
    h2 = _rms(x1, n2_ref[...]).astype(BF16)
    acc = x1
    for c in range(0, FFN_HIDDEN, FFN_TILE):
        gp = jnp.dot(h2, wfi_ref[:, c:c + FFN_TILE], preferred_element_type=F32)
        up = jnp.dot(h2, wfi_ref[:, FFN_HIDDEN + c:FFN_HIDDEN + c + FFN_TILE], preferred_element_type=F32)
        act = (gp * jax.nn.sigmoid(gp) * up).astype(BF16)
        acc = acc + jnp.dot(act, wfo_ref[c:c + FFN_TILE, :], preferred_element_type=F32)
    if final:
        acc = _rms(acc, fg_ref[...])
    o_ref[...] = acc


def _merge_ffn(x2, yp, ya, yr, gate, wup, wua, wur, wo, n2, wfi, wfo, fg, tm, final):
    T = x2.shape[0]
    row = lambda wd: pl.BlockSpec((tm, wd), lambda i: (i, 0))
    return pl.pallas_call(
        functools.partial(_merge_ffn_body, final=final),
        grid=(T // tm,),
        in_specs=[
            row(D_MODEL), row(POOL_WIDTH), row(ATT_WIDTH), row(REC_WIDTH), row(N_BRANCHES * D_MODEL),
            _resident((POOL_WIDTH, D_MODEL)), _resident((ATT_WIDTH, D_MODEL)), _resident((REC_WIDTH, D_MODEL)),
            _resident((D_MODEL, D_MODEL)), _resident((1, D_MODEL)),
            _resident((D_MODEL, 2 * FFN_HIDDEN)), _resident((FFN_HIDDEN, D_MODEL)), _resident((1, D_MODEL)),
        ],
        out_specs=row(D_MODEL),
        out_shape=jax.ShapeDtypeStruct((T, D_MODEL), F32),
        compiler_params=_params("arbitrary"),
        name="merge_ffn",
    )(x2, yp, ya, yr, gate, wup, wua, wur, wo, n2, wfi, wfo, fg)


def _diag_select():
    j = lax.broadcasted_iota(jnp.int32, (SUB * REC_K_DIM, CHUNK), 0) // REC_K_DIM
    c = lax.broadcasted_iota(jnp.int32, (SUB * REC_K_DIM, CHUNK), 1)
    return (c % SUB == j).astype(BF16)


@jax.jit
def kernel(x, norm1_g, w_in, pool_w, pool_scale, lam_q1, lam_k1, lam_q2, lam_k2, diff_norm_g, hgrn_lb,
           hgrn_norm_g, w_up_pool, w_up_attn, w_up_rec, w_out, norm2_g, w_ffn_in, w_ffn_out, final_norm_g):
    B, S, D = x.shape
    assert D == D_MODEL and w_in.shape[-1] == IN_COLS and S % CHUNK == 0
    T = B * S
    tm = min(512, T)
    qb = min(256, S)
    x2 = x.reshape(T, D).astype(F32)

    lb_all = jnp.cumsum(jax.nn.softmax(hgrn_lb.astype(F32), axis=1), axis=1)
    lb_all = lb_all - lb_all[:, :1]
    esel = _diag_select()
    row = lambda a: a.reshape(1, -1).astype(F32)

    for l in range(DEPTH):
        lambda_init = 0.8 - 0.6 * math.exp(-0.3 * l)
        (u_pool, aq, ak, av, rq, rf, rb, ri, rg, gate) = _inproj(
            x2, row(norm1_g[l]), w_in[l].astype(BF16), tm)
        y_pool = _pool(u_pool, pool_w[l].astype(BF16), row(pool_scale[l]), B, S)
        y_attn = _attn(aq, ak, av, row(lam_q1[l]), row(lam_k1[l]), row(lam_q2[l]), row(lam_k2[l]),
                       row(diff_norm_g[l]), B, S, qb, lambda_init)
        y_rec = _hgrn(rq, rf, rb, ri, rg, lb_all[:, l], row(hgrn_norm_g[l]), esel, B, S)
        x2 = _merge_ffn(x2, y_pool, y_attn, y_rec, gate,
                        w_up_pool[l].astype(BF16), w_up_attn[l].astype(BF16), w_up_rec[l].astype(BF16),
                        w_out[l].astype(BF16), row(norm2_g[l]), w_ffn_in[l].astype(BF16),
                        w_ffn_out[l].astype(BF16), row(final_norm_g), tm, l == DEPTH - 1)
    return x2.reshape(B, S, D).astype(x.dtype)
```

```python
import functools
import math

import jax
import jax.numpy as jnp
from jax import lax
from jax.experimental import pallas as pl
from jax.experimental.pallas import tpu as pltpu

F32 = jnp.float32
BF16 = jnp.bfloat16

D_MODEL = 1024
DEPTH = 4
N_POOL_GROUPS = 4
POOL_GROUP_DIM = 128
POOL_WIDTH = N_POOL_GROUPS * POOL_GROUP_DIM
POOL_WINDOWS = (2, 4, 8, 16)
ATT_HEADS = 4
ATT_QK_DIM = 64
ATT_V_DIM = 2 * ATT_QK_DIM
ATT_WIDTH = ATT_HEADS * ATT_V_DIM
REC_HEADS = 4
REC_K_DIM = 128
REC_V_DIM = 128
REC_WIDTH = REC_HEADS * REC_V_DIM
N_BRANCHES = 3
FFN_HIDDEN = ((math.ceil(8 * D_MODEL / 3) + 255) // 256) * 256
NORM_EPS = 1e-6
LOG2E = math.log2(math.e)

SUBLANES = 8
LANES = 128
VMEM_LIMIT_BYTES = 56 * 1024 * 1024

IN_SECTIONS = (
    ("pool", POOL_WIDTH, 1.0),
    ("aq", ATT_WIDTH, ATT_QK_DIM ** -0.5 * LOG2E),
    ("ak", ATT_WIDTH, 1.0),
    ("av", ATT_WIDTH, 1.0),
    ("rq", REC_WIDTH, REC_K_DIM ** -0.5),
    ("rf", REC_WIDTH, 1.0),
    ("rb", REC_WIDTH, 1.0),
    ("ri", REC_WIDTH, 1.0),
    ("rg", REC_WIDTH, 1.0),
    ("gate", N_BRANCHES * D_MODEL, 1.0),
)
IN_COLS = sum(s[1] for s in IN_SECTIONS)
IN_OUTPUTS = (
    ("pool", POOL_WIDTH, BF16), ("aq", ATT_WIDTH, BF16), ("ak", ATT_WIDTH, BF16), ("av", 2 * ATT_WIDTH, BF16),
    ("rq", REC_WIDTH, BF16),
    ("rf_split", 2 * REC_WIDTH, BF16), ("rf_logk", REC_WIDTH, F32),
    ("rb_split", 2 * REC_WIDTH, BF16), ("rb_logk", REC_WIDTH, F32),
    ("ri", REC_WIDTH, BF16), ("rg", REC_WIDTH, BF16), ("gate", N_BRANCHES * D_MODEL, BF16),
)

CHUNK = 128
SUB = SUBLANES
NSUB = CHUNK // SUB
LEVELS = tuple(2 * SUB * 2 ** i for i in range(int(math.log2(CHUNK // (2 * SUB))) + 1))
NFAC = 2 + 2 * len(LEVELS)
NEG_BIG = -1e30


def _params(*sem):
    return pltpu.CompilerParams(dimension_semantics=sem, vmem_limit_bytes=VMEM_LIMIT_BYTES)


def _resident(shape):
    return pl.BlockSpec(shape, lambda *_: (0,) * len(shape), pipeline_mode=pl.Buffered(1))


def _rms(x, g):
    return x * lax.rsqrt(jnp.mean(x * x, axis=-1, keepdims=True) + NORM_EPS) * g


def _log_gates(z, lb):
    a = jnp.log(lb)
    l1m = jnp.log1p(-lb)
    soft = jnp.log(1.0 + jnp.exp(-jnp.abs(z)))
    c = l1m + (jnp.minimum(z, 0.0) - soft)
    logf = jnp.maximum(a, c) + jnp.log(1.0 + jnp.exp(-jnp.abs(a - c)))
    logk = l1m - jnp.maximum(z, 0.0) - soft
    return logf * LOG2E, logk * LOG2E


GATE_ROWS = 64


def _inproj_body(x_ref, g_ref, w_ref, lb_ref, *refs, tm):
    out = {name: ref for ref, (name, _, _) in zip(refs, IN_OUTPUTS)}
    z_ref = refs[len(IN_OUTPUTS)]
    h = _rms(x_ref[...], g_ref[...]).astype(BF16)

    def gate_piece(name, d, r):
        rows = slice(r, r + GATE_ROWS)
        logf, logk = _log_gates(z_ref[d, rows, :], lb_ref[d:d + 1, :])
        hi = logf.astype(BF16)
        out[name + "_split"][rows, :REC_WIDTH] = hi
        out[name + "_split"][rows, REC_WIDTH:] = (logf - hi.astype(F32)).astype(BF16)
        out[name + "_logk"][rows, :] = logk

    always = pl.program_id(0) < pl.num_programs(0)
    pending = []
    off = 0
    for name, width, scale in IN_SECTIONS:
        for c in range(0, width, 512):
            def chunk(name=name, scale=scale, off=off, c=c, pieces=()):
                y = jnp.dot(h, w_ref[:, off + c:off + c + 512], preferred_element_type=F32)
                if scale != 1.0:
                    y = y * scale
                if name == "av":
                    o_ref = out["av"]
                    for hd in range(ATT_HEADS):
                        o_ref[:, 2 * hd * ATT_V_DIM:(2 * hd + 1) * ATT_V_DIM] = (
                            y[:, hd * ATT_V_DIM:(hd + 1) * ATT_V_DIM].astype(o_ref.dtype))
                        o_ref[:, (2 * hd + 1) * ATT_V_DIM:(2 * hd + 2) * ATT_V_DIM] = jnp.ones(
                            (y.shape[0], ATT_V_DIM), o_ref.dtype)
                elif name in ("rf", "rb"):
                    z_ref[0 if name == "rf" else 1] = y
                else:
                    out[name][:, c:c + 512] = y.astype(out[name].dtype)
                for piece in pieces:
                    gate_piece(*piece)

            if name in ("ri", "rg", "gate"):
                pieces, pending = pending[:2], pending[2:]
                pl.when(always)(functools.partial(chunk, pieces=tuple(pieces)))
            else:
                chunk()
            if name in ("rf", "rb"):
                pending += [(name, 0 if name == "rf" else 1, r) for r in range(0, tm, GATE_ROWS)]
        off += width
    assert not pending


def _inproj(x2, g, w, lb, tm):
    T = x2.shape[0]
    return pl.pallas_call(
        functools.partial(_inproj_body, tm=tm),
        grid=(T // tm,),
        in_specs=[
            pl.BlockSpec((tm, D_MODEL), lambda i: (i, 0)),
            _resident((1, D_MODEL)),
            _resident((D_MODEL, IN_COLS)),
            _resident((2, REC_WIDTH)),
        ],
        out_specs=[pl.BlockSpec((tm, wd), lambda i: (i, 0)) for _, wd, _ in IN_OUTPUTS],
        out_shape=[jax.ShapeDtypeStruct((T, wd), dt) for _, wd, dt in IN_OUTPUTS],
        scratch_shapes=[pltpu.VMEM((2, tm, REC_WIDTH), F32)],
        compiler_params=_params("arbitrary"),
        name="inproj",
    )(x2, g, w, lb)


def _pool_body(u_ref, w_ref, sc_ref, o_ref, *, S):
    t = lax.broadcasted_iota(jnp.int32, (S, POOL_GROUP_DIM), 0)
    for g, win in enumerate(POOL_WINDOWS):
        cols = slice(g * POOL_GROUP_DIM, (g + 1) * POOL_GROUP_DIM)
        u = u_ref[:, cols].astype(F32)
        tot = u
        for o in range(-(win // 2), win // 2):
            if o == 0:
                continue
            shifted = pltpu.roll(u, (-o) % S, axis=0)
            valid = (t + o >= 0) & (t + o < S)
            tot = tot + jnp.where(valid, shifted, 0.0)
        lo = jnp.maximum(t - win // 2, 0)
        hi = jnp.minimum(t + win // 2 - 1, S - 1)
        cnt = (hi - lo + 1).astype(F32)
        d = (tot / cnt - u).astype(BF16)
        y = jnp.dot(d, w_ref[g], preferred_element_type=F32) * sc_ref[:, cols]
        o_ref[:, cols] = y.astype(o_ref.dtype)


def _pool(u, w, sc, B, S):
    return pl.pallas_call(
        functools.partial(_pool_body, S=S),
        grid=(B,),
        in_specs=[
            pl.BlockSpec((S, POOL_WIDTH), lambda b: (b, 0)),
            _resident((N_POOL_GROUPS, POOL_GROUP_DIM, POOL_GROUP_DIM)),
            _resident((1, POOL_WIDTH)),
        ],
        out_specs=pl.BlockSpec((S, POOL_WIDTH), lambda b: (b, 0)),
        out_shape=jax.ShapeDtypeStruct((B * S, POOL_WIDTH), BF16),
        compiler_params=_params("arbitrary"),
        name="pool",
    )(u, w, sc)


_NT = (((1,), (1,)), ((), ()))
_TN = (((0,), (0,)), ((), ()))


def _attn_body(q_ref, k_ref, v_ref, lq1_ref, lk1_ref, lq2_ref, lk2_ref, gain_ref, o_ref, s_ref, bias_ref,
               *, S, qb, lambda_init):
    qi = pl.program_id(1)
    lam = (jnp.exp(jnp.sum(lq1_ref[...] * lk1_ref[...], axis=-1, keepdims=True))
           - jnp.exp(jnp.sum(lq2_ref[...] * lk2_ref[...], axis=-1, keepdims=True))
           + lambda_init)
    qpos = qi * qb + lax.broadcasted_iota(jnp.int32, (qb, S), 0)
    kpos = lax.broadcasted_iota(jnp.int32, (qb, S), 1)
    dist = jnp.abs(qpos - kpos).astype(F32)
    lane = lax.broadcasted_iota(jnp.int32, (qb, ATT_V_DIM), 1)

    for h in range(ATT_HEADS):
        cols = slice(h * ATT_V_DIM, (h + 1) * ATT_V_DIM)
        qh = q_ref[:, cols]
        zero = jnp.zeros_like(qh)
        qs = jnp.concatenate([jnp.where(lane < ATT_QK_DIM, qh, zero),
                              jnp.where(lane >= ATT_QK_DIM, qh, zero)], axis=0)
        s_ref[h] = lax.dot_general(qs, k_ref[:, cols], _NT, preferred_element_type=F32)
        bias_ref[h] = dist * (2.0 ** (-8.0 * (h + 1) / ATT_HEADS) * LOG2E)
    for h in range(ATT_HEADS):
        cols = slice(h * ATT_V_DIM, (h + 1) * ATT_V_DIM)
        v_ext = v_ref[:, 2 * h * ATT_V_DIM:(2 * h + 2) * ATT_V_DIM]
        sm = []
        for c in range(2):
            rows = slice(c * qb, (c + 1) * qb)
            m = jnp.max(s_ref[h, rows, :] - bias_ref[h], axis=-1, keepdims=True)
            p = jnp.exp2(s_ref[h, rows, :] - bias_ref[h] - m).astype(BF16)
            ol = jnp.dot(p, v_ext, preferred_element_type=F32)
            sm.append(ol[:, :ATT_V_DIM] / ol[:, ATT_V_DIM:])
        o = _rms(sm[0] - lam * sm[1], gain_ref[...]) * (1.0 - lambda_init)
        o_ref[:, cols] = o.astype(o_ref.dtype)


def _attn(q, k, v, lq1, lk1, lq2, lk2, gain, B, S, qb, lambda_init):
    small = _resident((1, ATT_QK_DIM))
    return pl.pallas_call(
        functools.partial(_attn_body, S=S, qb=qb, lambda_init=lambda_init),
        grid=(B, S // qb),
        in_specs=[
            pl.BlockSpec((qb, ATT_WIDTH), lambda b, i: (b * (S // qb) + i, 0)),
            pl.BlockSpec((S, ATT_WIDTH), lambda b, i: (b, 0)),
            pl.BlockSpec((S, 2 * ATT_WIDTH), lambda b, i: (b, 0)),
            small, small, small, small,
            _resident((1, ATT_V_DIM)),
        ],
        out_specs=pl.BlockSpec((qb, ATT_WIDTH), lambda b, i: (b * (S // qb) + i, 0)),
        out_shape=jax.ShapeDtypeStruct((B * S, ATT_WIDTH), BF16),
        scratch_shapes=[pltpu.VMEM((ATT_HEADS, 2 * qb, S), F32),
                        pltpu.VMEM((ATT_HEADS, qb, S), F32)],
        compiler_params=_params("arbitrary", "arbitrary"),
        name="diffattn",
    )(q, k, v, lq1, lk1, lq2, lk2, gain)


HBLOCK = 1024
HSTEPS = HBLOCK // CHUNK


def _same_block(r, c, size):
    return ((r ^ c) & ~(size - 1)) == 0


def _rows_bcast(ref, rows, n):
    parts = [jnp.broadcast_to(ref[r:r + 1, :], (n, LANES)) for r in rows]
    return parts[0] if len(parts) == 1 else jnp.concatenate(parts, axis=0)


def _sub_block_factors(b_ref, fac_ref, rev, sub):
    be = b_ref[pl.ds(0 if rev else SUB - 1, NSUB, stride=SUB), :]
    if rev:
        bs = jnp.where(sub == NSUB - 1, 0.0, pltpu.roll(be, NSUB - 1, axis=0))
        last = 0
    else:
        bs = jnp.where(sub == 0, 0.0, pltpu.roll(be, 1, axis=0))
        last = CHUNK - 1
    b_last = b_ref[last:last + 1, :]
    fac_ref[0:NSUB, :] = jnp.exp2(bs)
    fac_ref[NSUB:2 * NSUB, :] = jnp.exp2(b_last - be)
    for j, m in enumerate(LEVELS):
        per = m // SUB
        mids = [m * blk + (m // 2 if rev else m // 2 - 1) for blk in range(CHUNK // m)]
        bmid = jnp.broadcast_to(b_ref[mids[-1]:mids[-1] + 1, :], (NSUB, LANES))
        for blk in range(CHUNK // m - 2, -1, -1):
            row = jnp.broadcast_to(b_ref[mids[blk]:mids[blk] + 1, :], (NSUB, LANES))
            bmid = jnp.where(sub < (blk + 1) * per, row, bmid)
        pos = sub & (per - 1)
        late = (pos < per // 2) if rev else (pos >= per // 2)
        fac_ref[(2 + 2 * j) * NSUB:(3 + 2 * j) * NSUB, :] = jnp.exp2(jnp.where(late, bs - bmid, NEG_BIG))
        fac_ref[(3 + 2 * j) * NSUB:(4 + 2 * j) * NSUB, :] = jnp.exp2(jnp.where(late, NEG_BIG, bmid - be))
    return b_last


def _expand(fac_ref, f):
    return _rows_bcast(fac_ref, [f * NSUB + i for i in range(NSUB)], SUB)


def _chunk_decays(r0, split_ref, logk_ref, tri):
    cs = jnp.dot(tri, split_ref[pl.ds(r0, CHUNK), :], preferred_element_type=F32)
    b_all = cs[:, :REC_WIDTH] + cs[:, REC_WIDTH:]
    return b_all, b_all - logk_ref[pl.ds(r0, CHUNK), :]


def _chunk_operands(r0, d, rev, b_all, c_all, q_ref, esel_ref, b_ref, c_ref, fac_ref, sub):
    starts = ([SUB * (i + 1) for i in range(NSUB - 1)] + [None]) if rev else ([None] + [SUB * i - 1 for i in range(1, NSUB)])
    ends = [SUB * i for i in range(NSUB)] if rev else [SUB * i + SUB - 1 for i in range(NSUB)]
    heads, stacked = [], []
    for h in range(REC_HEADS):
        cols = slice(h * REC_K_DIM, (h + 1) * REC_K_DIM)
        bh, ch, fh = b_ref.at[d, h], c_ref.at[d, h], fac_ref.at[d, h]
        b = b_all[:, cols]
        c = c_all[:, cols]
        bh[...] = b
        ch[...] = c
        b_last = _sub_block_factors(bh, fh, rev, sub)
        q = q_ref[pl.ds(r0, CHUNK), cols].astype(F32)
        be = _rows_bcast(bh, ends, SUB)
        bs_parts = [jnp.zeros((SUB, LANES), F32) if r is None else jnp.broadcast_to(bh[r:r + 1, :], (SUB, LANES))
                    for r in starts]
        bs = jnp.concatenate(bs_parts, axis=0)
        q_hat = q * jnp.exp2(b - bs)
        k_hat = jnp.exp2(be - c)
        pieces = []
        for sp in range(SUB):
            crow = _rows_bcast(ch, [SUB * i + sp for i in range(NSUB)], SUB)
            pieces.append((q * jnp.exp2(jnp.minimum(b - crow, 0.0))).astype(BF16))
        stacked.append(jnp.concatenate(pieces, axis=1))
        heads.append((b_last, q_hat, k_hat))
    a_diag = jnp.dot(jnp.concatenate(stacked, axis=0), esel_ref[...], preferred_element_type=F32)
    return heads, a_diag


def _chunk_state_and_scores(r0, d, heads, a_diag, v_ref, st_ref, fac_ref, dmask, same_sub, lmasks):
    out = []
    for h, (b_last, q_hat, k_hat) in enumerate(heads):
        cols = slice(h * REC_K_DIM, (h + 1) * REC_K_DIM)
        fh = fac_ref.at[d, h]
        v = v_ref[pl.ds(r0, CHUNK), cols]
        st = st_ref[d, h]
        o = lax.dot_general((q_hat * _expand(fh, 0)).astype(BF16), st.astype(BF16), _NT,
                            preferred_element_type=F32)
        k_up = (k_hat * _expand(fh, 1)).astype(BF16)
        st_ref[d, h] = st * jnp.exp2(b_last) + lax.dot_general(v, k_up, _TN, preferred_element_type=F32)
        a = None
        for j in reversed(range(len(LEVELS))):
            q_m = (q_hat * _expand(fh, 2 + 2 * j)).astype(BF16)
            k_m = (k_hat * _expand(fh, 3 + 2 * j)).astype(BF16)
            a_m = lax.dot_general(q_m, k_m, _NT, preferred_element_type=F32)
            a = a_m if a is None else jnp.where(lmasks[j], a_m, a)
        a = jnp.where(same_sub, 0.0, a)
        a = jnp.where(dmask, a_diag[h * CHUNK:(h + 1) * CHUNK], a)
        out.append((o, a.astype(BF16), v))
    return out


def _chunk_outputs(r0, parts, acc_ref):
    for h, (o, a, v) in enumerate(parts):
        cols = slice(h * REC_V_DIM, (h + 1) * REC_V_DIM)
        acc_ref[pl.ds(r0, CHUNK), cols] = o + jnp.dot(a, v, preferred_element_type=F32)


def _hgrn_body(qf_ref, qb_ref, sf_ref, kf_ref, sb_ref, kb_ref, vf_ref, vb_ref, g_ref, gain_ref, esel_ref, o_ref,
               st_ref, b_ref, c_ref, fac_ref, accf_ref, accb_ref, *, S):
    nj = S // HBLOCK
    j = pl.program_id(1)

    @pl.when(j == 0)
    def _():
        st_ref[...] = jnp.zeros_like(st_ref)

    sub = lax.broadcasted_iota(jnp.int32, (NSUB, LANES), 0)
    rt = lax.broadcasted_iota(jnp.int32, (CHUNK, CHUNK), 0)
    ct = lax.broadcasted_iota(jnp.int32, (CHUNK, CHUNK), 1)
    lmasks = [_same_block(rt, ct, m) for m in LEVELS]
    same_sub = _same_block(rt, ct, SUB)
    dmasks = (same_sub & (rt >= ct), same_sub & (rt <= ct))
    tris = (jnp.where(ct <= rt, 1.0, 0.0).astype(BF16), jnp.where(ct >= rt, 1.0, 0.0).astype(BF16))
    q_refs, split_refs, logk_refs, v_refs = (qf_ref, qb_ref), (sf_ref, sb_ref), (kf_ref, kb_ref), (vf_ref, vb_ref)
    acc_refs = (accf_ref, accb_ref)
    base = (j * HBLOCK, (nj - 1 - j) * HBLOCK)

    def step(i, carry):
        r0 = (pl.multiple_of(i * CHUNK, CHUNK), pl.multiple_of((HSTEPS - 1 - i) * CHUNK, CHUNK))
        decays = [_chunk_decays(r0[d], split_refs[d], logk_refs[d], tris[d]) for d in range(2)]
        ops = [_chunk_operands(r0[d], d, d == 1, *decays[d], q_refs[d], esel_ref, b_ref, c_ref, fac_ref, sub)
               for d in range(2)]
        parts = [_chunk_state_and_scores(r0[d], d, *ops[d], v_refs[d], st_ref, fac_ref, dmasks[d], same_sub,
                                         lmasks) for d in range(2)]
        for d in range(2):
            _chunk_outputs(pl.multiple_of(base[d] + r0[d], CHUNK), parts[d], acc_refs[d])
        return carry

    lax.fori_loop(0, HSTEPS, step, 0)

    @pl.when(j == nj - 1)
    def _():
        for h in range(REC_HEADS):
            cols = slice(h * REC_V_DIM, (h + 1) * REC_V_DIM)
            o = _rms(accf_ref[:, cols] + accb_ref[:, cols], gain_ref[...])
            g = g_ref[:, cols].astype(F32)
            o_ref[:, cols] = (o * (g * jax.nn.sigmoid(g))).astype(o_ref.dtype)


def _hgrn(q, split_f, logk_f, split_b, logk_b, v, g, gain, esel, B, S):
    nj = S // HBLOCK
    fwd = lambda wd: pl.BlockSpec((HBLOCK, wd), lambda b, j: (b * nj + j, 0))
    bwd = lambda wd: pl.BlockSpec((HBLOCK, wd), lambda b, j: (b * nj + nj - 1 - j, 0))
    seq = lambda: pl.BlockSpec((S, REC_WIDTH), lambda b, j: (b, 0))
    return pl.pallas_call(
        functools.partial(_hgrn_body, S=S),
        grid=(B, nj),
        in_specs=[
            fwd(REC_WIDTH), bwd(REC_WIDTH),
            fwd(2 * REC_WIDTH), fwd(REC_WIDTH), bwd(2 * REC_WIDTH), bwd(REC_WIDTH),
            fwd(REC_WIDTH), bwd(REC_WIDTH),
            seq(),
            _resident((1, REC_V_DIM)),
            _resident((SUB * REC_K_DIM, CHUNK)),
        ],
        out_specs=seq(),
        out_shape=jax.ShapeDtypeStruct((B * S, REC_WIDTH), BF16),
        scratch_shapes=[
            pltpu.VMEM((2, REC_HEADS, REC_V_DIM, REC_K_DIM), F32),
            pltpu.VMEM((2, REC_HEADS, CHUNK, REC_K_DIM), F32),
            pltpu.VMEM((2, REC_HEADS, CHUNK, REC_K_DIM), F32),
            pltpu.VMEM((2, REC_HEADS, NFAC * NSUB, REC_K_DIM), F32),
            pltpu.VMEM((S, REC_WIDTH), F32),
            pltpu.VMEM((S, REC_WIDTH), F32),
        ],
        compiler_params=_params("arbitrary", "arbitrary"),
        name="hgrn2",
    )(q, q, split_f, logk_f, split_b, logk_b, v, v, g, gain, esel)


FFN_TILE = 256


def _merge_ffn_body(x_ref, yp_ref, ya_ref, yr_ref, gate_ref, wup_ref, wua_ref, wur_ref, wo_ref,
                    n2_ref, wfi_ref, wfo_ref, fg_ref, o_ref, *, final):
    merged = None
    for k, (y_ref, w_ref) in enumerate(((yp_ref, wup_ref), (ya_ref, wua_ref), (yr_ref, wur_ref))):
        gate = jax.nn.sigmoid(gate_ref[:, k * D_MODEL:(k + 1) * D_MODEL].astype(F32))
        term = gate * jnp.dot(y_ref[...], w_ref[...], preferred_element_type=F32)
        merged = term if merged is None else merged + term
    x1 = x_ref[...] + jnp.dot(merged.astype(BF16), wo_ref[...], preferred_element_type=F32)
    h2 = _rms(x1, n2_ref[...]).astype(BF16)
    acc = x1
    for c in range(0, FFN_HIDDEN, FFN_TILE):
        gp = jnp.dot(h2, wfi_ref[:, c:c + FFN_TILE], preferred_element_type=F32)
        up = jnp.dot(h2, wfi_ref[:, FFN_HIDDEN + c:FFN_HIDDEN + c + FFN_TILE], preferred_element_type=F32)
        act = (gp * jax.nn.sigmoid(gp) * up).astype(BF16)
        acc = acc + jnp.dot(act, wfo_ref[c:c + FFN_TILE, :], preferred_element_type=F32)
    if final:
        acc = _rms(acc, fg_ref[...])
    o_ref[...] = acc


def _merge_ffn(x2, yp, ya, yr, gate, wup, wua, wur, wo, n2, wfi, wfo, fg, tm, final):
    T = x2.shape[0]
    row = lambda wd: pl.BlockSpec((tm, wd), lambda i: (i, 0))
    return pl.pallas_call(
        functools.partial(_merge_ffn_body, final=final),
        grid=(T // tm,),
        in_specs=[
            row(D_MODEL), row(POOL_WIDTH), row(ATT_WIDTH), row(REC_WIDTH), row(N_BRANCHES * D_MODEL),
            _resident((POOL_WIDTH, D_MODEL)), _resident((ATT_WIDTH, D_MODEL)), _resident((REC_WIDTH, D_MODEL)),
            _resident((D_MODEL, D_MODEL)), _resident((1, D_MODEL)),
            _resident((D_MODEL, 2 * FFN_HIDDEN)), _resident((FFN_HIDDEN, D_MODEL)), _resident((1, D_MODEL)),
        ],
        out_specs=row(D_MODEL),
        out_shape=jax.ShapeDtypeStruct((T, D_MODEL), F32),
        compiler_params=_params("arbitrary"),
        name="merge_ffn",
    )(x2, yp, ya, yr, gate, wup, wua, wur, wo, n2, wfi, wfo, fg)


def _diag_select():
    j = lax.broadcasted_iota(jnp.int32, (SUB * REC_K_DIM, CHUNK), 0) // REC_K_DIM
    c = lax.broadcasted_iota(jnp.int32, (SUB * REC_K_DIM, CHUNK), 1)
    return (c % SUB == j).astype(BF16)


@jax.jit
def kernel(x, norm1_g, w_in, pool_w, pool_scale, lam_q1, lam_k1, lam_q2, lam_k2, diff_norm_g, hgrn_lb,
           hgrn_norm_g, w_up_pool, w_up_attn, w_up_rec, w_out, norm2_g, w_ffn_in, w_ffn_out, final_norm_g):
    B, S, D = x.shape
    assert D == D_MODEL and w_in.shape[-1] == IN_COLS and S % HBLOCK == 0
    T = B * S
    tm = min(512, T)
    qb = min(256, S)
    x2 = x.reshape(T, D).astype(F32)

    lb_all = jnp.cumsum(jax.nn.softmax(hgrn_lb.astype(F32), axis=1), axis=1)
    lb_all = lb_all - lb_all[:, :1]
    esel = _diag_select()
    row = lambda a: a.reshape(1, -1).astype(F32)

    for l in range(DEPTH):
        lambda_init = 0.8 - 0.6 * math.exp(-0.3 * l)
        (u_pool, aq, ak, av, rq, split_f, logk_f, split_b, logk_b, ri, rg, gate) = _inproj(
            x2, row(norm1_g[l]), w_in[l].astype(BF16), lb_all[:, l], tm)
        y_pool = _pool(u_pool, pool_w[l].astype(BF16), row(pool_scale[l]), B, S)
        y_attn = _attn(aq, ak, av, row(lam_q1[l]), row(lam_k1[l]), row(lam_q2[l]), row(lam_k2[l]),
                       row(diff_norm_g[l]), B, S, qb, lambda_init)
        y_rec = _hgrn(rq, split_f, logk_f, split_b, logk_b, ri, rg, row(hgrn_norm_g[l]), esel, B, S)
        x2 = _merge_ffn(x2, y_pool, y_attn, y_rec, gate,
                        w_up_pool[l].astype(BF16), w_up_attn[l].astype(BF16), w_up_rec[l].astype(BF16),
                        w_out[l].astype(BF16), row(norm2_g[l]), w_ffn_in[l].astype(BF16),
                        w_ffn_out[l].astype(BF16), row(final_norm_g), tm, l == DEPTH - 1)
    return x2.reshape(B, S, D).astype(x.dtype)
```

```python
import functools
import math

import jax
import jax.numpy as jnp
from jax import lax
from jax.experimental import pallas as pl
from jax.experimental.pallas import tpu as pltpu

F32 = jnp.float32
BF16 = jnp.bfloat16

D_MODEL = 1024
DEPTH = 4
N_POOL_GROUPS = 4
POOL_GROUP_DIM = 128
POOL_WIDTH = N_POOL_GROUPS * POOL_GROUP_DIM
POOL_WINDOWS = (2, 4, 8, 16)
ATT_HEADS = 4
ATT_QK_DIM = 64
ATT_V_DIM = 2 * ATT_QK_DIM
ATT_WIDTH = ATT_HEADS * ATT_V_DIM
REC_HEADS = 4
REC_K_DIM = 128
REC_V_DIM = 128
REC_WIDTH = REC_HEADS * REC_V_DIM
N_BRANCHES = 3
FFN_HIDDEN = ((math.ceil(8 * D_MODEL / 3) + 255) // 256) * 256
NORM_EPS = 1e-6
LOG2E = math.log2(math.e)

SUBLANES = 8
LANES = 128
VMEM_LIMIT_BYTES = 56 * 1024 * 1024

IN_SECTIONS = (
    ("pool", POOL_WIDTH, 1.0),
    ("aq", ATT_WIDTH, ATT_QK_DIM ** -0.5 * LOG2E),
    ("ak", ATT_WIDTH, 1.0),
    ("av", ATT_WIDTH, 1.0),
    ("rq", REC_WIDTH, REC_K_DIM ** -0.5),
    ("rf", REC_WIDTH, 1.0),
    ("rb", REC_WIDTH, 1.0),
    ("ri", REC_WIDTH, 1.0),
    ("rg", REC_WIDTH, 1.0),
    ("gate", N_BRANCHES * D_MODEL, 1.0),
)
IN_COLS = sum(s[1] for s in IN_SECTIONS)
IN_OUTPUTS = (
    ("pool", POOL_WIDTH, BF16), ("aq", ATT_WIDTH, BF16), ("ak", ATT_WIDTH, BF16), ("av", 2 * ATT_WIDTH, BF16),
    ("rq", REC_WIDTH, BF16), ("rf", REC_WIDTH, F32), ("rb", REC_WIDTH, F32),
    ("ri", REC_WIDTH, BF16), ("rg", REC_WIDTH, BF16), ("gate", N_BRANCHES * D_MODEL, BF16),
)

CHUNK = 128
SUB = SUBLANES
NSUB = CHUNK // SUB
LEVELS = tuple(2 * SUB * 2 ** i for i in range(int(math.log2(CHUNK // (2 * SUB))) + 1))
NFAC = 2 + 2 * len(LEVELS)
NEG_BIG = -1e30


def _params(*sem):
    return pltpu.CompilerParams(dimension_semantics=sem, vmem_limit_bytes=VMEM_LIMIT_BYTES)


def _resident(shape):
    return pl.BlockSpec(shape, lambda *_: (0,) * len(shape), pipeline_mode=pl.Buffered(1))


def _rms(x, g):
    return x * lax.rsqrt(jnp.mean(x * x, axis=-1, keepdims=True) + NORM_EPS) * g


def _log_gates(z, lb):
    a = jnp.log(lb)
    l1m = jnp.log1p(-lb)
    soft = jnp.log(1.0 + jnp.exp(-jnp.abs(z)))
    c = l1m + (jnp.minimum(z, 0.0) - soft)
    logf = jnp.maximum(a, c) + jnp.log(1.0 + jnp.exp(-jnp.abs(a - c)))
    logk = l1m - jnp.maximum(z, 0.0) - soft
    return logf * LOG2E, logk * LOG2E


def _inproj_body(x_ref, g_ref, w_ref, *refs):
    out = {name: ref for ref, (name, _, _) in zip(refs, IN_OUTPUTS)}
    h = _rms(x_ref[...], g_ref[...]).astype(BF16)
    off = 0
    for name, width, scale in IN_SECTIONS:
        for c in range(0, width, 512):
            y = jnp.dot(h, w_ref[:, off + c:off + c + 512], preferred_element_type=F32)
            if scale != 1.0:
                y = y * scale
            if name == "av":
                o_ref = out["av"]
                for hd in range(ATT_HEADS):
                    o_ref[:, 2 * hd * ATT_V_DIM:(2 * hd + 1) * ATT_V_DIM] = (
                        y[:, hd * ATT_V_DIM:(hd + 1) * ATT_V_DIM].astype(o_ref.dtype))
                    o_ref[:, (2 * hd + 1) * ATT_V_DIM:(2 * hd + 2) * ATT_V_DIM] = jnp.ones(
                        (y.shape[0], ATT_V_DIM), o_ref.dtype)
            else:
                out[name][:, c:c + 512] = y.astype(out[name].dtype)
        off += width


def _inproj(x2, g, w, tm):
    T = x2.shape[0]
    return pl.pallas_call(
        _inproj_body,
        grid=(T // tm,),
        in_specs=[
            pl.BlockSpec((tm, D_MODEL), lambda i: (i, 0)),
            _resident((1, D_MODEL)),
            _resident((D_MODEL, IN_COLS)),
        ],
        out_specs=[pl.BlockSpec((tm, wd), lambda i: (i, 0)) for _, wd, _ in IN_OUTPUTS],
        out_shape=[jax.ShapeDtypeStruct((T, wd), dt) for _, wd, dt in IN_OUTPUTS],
        compiler_params=_params("arbitrary"),
        name="inproj",
    )(x2, g, w)


POOL_PAD = SUBLANES


def _pool_body(u_ref, w_ref, sc_ref, icnt_ref, o_ref, *, S):
    n = S + 2 * POOL_PAD
    zpad = jnp.zeros((POOL_PAD, POOL_GROUP_DIM), F32)
    for g, win in enumerate(POOL_WINDOWS):
        cols = slice(g * POOL_GROUP_DIM, (g + 1) * POOL_GROUP_DIM)
        u = u_ref[:, cols].astype(F32)
        p = jnp.concatenate([zpad, u, zpad], axis=0)
        sh = 1
        while sh < win:
            p = p + pltpu.roll(p, sh, axis=0)
            sh *= 2
        ahead = win // 2 - 1
        if ahead:
            p = pltpu.roll(p, n - ahead, axis=0)
        tot = p[POOL_PAD:POOL_PAD + S]
        d = (tot * icnt_ref[g] - u).astype(BF16)
        y = jnp.dot(d, w_ref[g], preferred_element_type=F32) * sc_ref[:, cols]
        o_ref[:, cols] = y.astype(o_ref.dtype)


def _pool(u, w, sc, B, S):
    t = jnp.arange(S)
    icnt = jnp.stack([1.0 / (jnp.minimum(t + win // 2 - 1, S - 1) - jnp.maximum(t - win // 2, 0) + 1).astype(F32)
                      for win in POOL_WINDOWS])
    icnt = jnp.broadcast_to(icnt[:, :, None], (N_POOL_GROUPS, S, POOL_GROUP_DIM))
    return pl.pallas_call(
        functools.partial(_pool_body, S=S),
        grid=(B,),
        in_specs=[
            pl.BlockSpec((S, POOL_WIDTH), lambda b: (b, 0)),
            _resident((N_POOL_GROUPS, POOL_GROUP_DIM, POOL_GROUP_DIM)),
            _resident((1, POOL_WIDTH)),
            _resident((N_POOL_GROUPS, S, POOL_GROUP_DIM)),
        ],
        out_specs=pl.BlockSpec((S, POOL_WIDTH), lambda b: (b, 0)),
        out_shape=jax.ShapeDtypeStruct((B * S, POOL_WIDTH), BF16),
        compiler_params=_params("arbitrary"),
        name="pool",
    )(u, w, sc, icnt)


_NT = (((1,), (1,)), ((), ()))
_TN = (((0,), (0,)), ((), ()))


def _attn_body(q_ref, k_ref, v_ref, lq1_ref, lk1_ref, lq2_ref, lk2_ref, gain_ref, o_ref, s_ref, bias_ref,
               *, S, qb, lambda_init):
    qi = pl.program_id(1)
    lam = (jnp.exp(jnp.sum(lq1_ref[...] * lk1_ref[...], axis=-1, keepdims=True))
           - jnp.exp(jnp.sum(lq2_ref[...] * lk2_ref[...], axis=-1, keepdims=True))
           + lambda_init)
    qpos = qi * qb + lax.broadcasted_iota(jnp.int32, (qb, S), 0)
    kpos = lax.broadcasted_iota(jnp.int32, (qb, S), 1)
    dist = jnp.abs(qpos - kpos).astype(F32)
    lane = lax.broadcasted_iota(jnp.int32, (qb, ATT_V_DIM), 1)

    for h in range(ATT_HEADS):
        cols = slice(h * ATT_V_DIM, (h + 1) * ATT_V_DIM)
        qh = q_ref[:, cols]
        zero = jnp.zeros_like(qh)
        qs = jnp.concatenate([jnp.where(lane < ATT_QK_DIM, qh, zero),
                              jnp.where(lane >= ATT_QK_DIM, qh, zero)], axis=0)
        s_ref[h] = lax.dot_general(qs, k_ref[:, cols], _NT, preferred_element_type=F32)
        bias_ref[h] = dist * (2.0 ** (-8.0 * (h + 1) / ATT_HEADS) * LOG2E)
    for h in range(ATT_HEADS):
        cols = slice(h * ATT_V_DIM, (h + 1) * ATT_V_DIM)
        v_ext = v_ref[:, 2 * h * ATT_V_DIM:(2 * h + 2) * ATT_V_DIM]
        sm = []
        for c in range(2):
            rows = slice(c * qb, (c + 1) * qb)
            m = jnp.max(s_ref[h, rows, :] - bias_ref[h], axis=-1, keepdims=True)
            p = jnp.exp2(s_ref[h, rows, :] - bias_ref[h] - m).astype(BF16)
            ol = jnp.dot(p, v_ext, preferred_element_type=F32)
            sm.append(ol[:, :ATT_V_DIM] / ol[:, ATT_V_DIM:])
        o = _rms(sm[0] - lam * sm[1], gain_ref[...]) * (1.0 - lambda_init)
        o_ref[:, cols] = o.astype(o_ref.dtype)


def _attn(q, k, v, lq1, lk1, lq2, lk2, gain, B, S, qb, lambda_init):
    small = _resident((1, ATT_QK_DIM))
    return pl.pallas_call(
        functools.partial(_attn_body, S=S, qb=qb, lambda_init=lambda_init),
        grid=(B, S // qb),
        in_specs=[
            pl.BlockSpec((qb, ATT_WIDTH), lambda b, i: (b * (S // qb) + i, 0)),
            pl.BlockSpec((S, ATT_WIDTH), lambda b, i: (b, 0)),
            pl.BlockSpec((S, 2 * ATT_WIDTH), lambda b, i: (b, 0)),
            small, small, small, small,
            _resident((1, ATT_V_DIM)),
        ],
        out_specs=pl.BlockSpec((qb, ATT_WIDTH), lambda b, i: (b * (S // qb) + i, 0)),
        out_shape=jax.ShapeDtypeStruct((B * S, ATT_WIDTH), BF16),
        scratch_shapes=[pltpu.VMEM((ATT_HEADS, 2 * qb, S), F32),
                        pltpu.VMEM((ATT_HEADS, qb, S), F32)],
        compiler_params=_params("arbitrary", "arbitrary"),
        name="diffattn",
    )(q, k, v, lq1, lk1, lq2, lk2, gain)


HBLOCK = 1024
HSTEPS = HBLOCK // CHUNK
HUNROLL = 2


def _same_block(r, c, size):
    return ((r ^ c) & ~(size - 1)) == 0


def _rows_bcast(ref, rows, n):
    parts = [jnp.broadcast_to(ref[r:r + 1, :], (n, LANES)) for r in rows]
    return parts[0] if len(parts) == 1 else jnp.concatenate(parts, axis=0)


def _sub_block_factors(b_ref, fac_ref, rev, sub):
    be = b_ref[pl.ds(0 if rev else SUB - 1, NSUB, stride=SUB), :]
    if rev:
        bs = jnp.where(sub == NSUB - 1, 0.0, pltpu.roll(be, NSUB - 1, axis=0))
        last = 0
    else:
        bs = jnp.where(sub == 0, 0.0, pltpu.roll(be, 1, axis=0))
        last = CHUNK - 1
    b_last = b_ref[last:last + 1, :]
    fac_ref[0:NSUB, :] = jnp.exp2(bs)
    fac_ref[NSUB:2 * NSUB, :] = jnp.exp2(b_last - be)
    for j, m in enumerate(LEVELS):
        per = m // SUB
        mids = [m * blk + (m // 2 if rev else m // 2 - 1) for blk in range(CHUNK // m)]
        bmid = jnp.broadcast_to(b_ref[mids[-1]:mids[-1] + 1, :], (NSUB, LANES))
        for blk in range(CHUNK // m - 2, -1, -1):
            row = jnp.broadcast_to(b_ref[mids[blk]:mids[blk] + 1, :], (NSUB, LANES))
            bmid = jnp.where(sub < (blk + 1) * per, row, bmid)
        pos = sub & (per - 1)
        late = (pos < per // 2) if rev else (pos >= per // 2)
        fac_ref[(2 + 2 * j) * NSUB:(3 + 2 * j) * NSUB, :] = jnp.exp2(jnp.where(late, bs - bmid, NEG_BIG))
        fac_ref[(3 + 2 * j) * NSUB:(4 + 2 * j) * NSUB, :] = jnp.exp2(jnp.where(late, NEG_BIG, bmid - be))
    return b_last


def _expand(fac_ref, f):
    return _rows_bcast(fac_ref, [f * NSUB + i for i in range(NSUB)], SUB)


def _chunk_decays(r0, z_ref, lb, tri):
    logf, logk = _log_gates(z_ref[pl.ds(r0, CHUNK), :], lb)
    hi = logf.astype(BF16)
    lo = (logf - hi.astype(F32)).astype(BF16)
    cs = jnp.dot(tri, jnp.concatenate([hi, lo], axis=1), preferred_element_type=F32)
    b_all = cs[:, :REC_WIDTH] + cs[:, REC_WIDTH:]
    return b_all, b_all - logk


def _chunk_operands(r0, d, rev, b_all, c_all, q_ref, esel_ref, b_ref, c_ref, fac_ref, sub):
    starts = ([SUB * (i + 1) for i in range(NSUB - 1)] + [None]) if rev else ([None] + [SUB * i - 1 for i in range(1, NSUB)])
    ends = [SUB * i for i in range(NSUB)] if rev else [SUB * i + SUB - 1 for i in range(NSUB)]
    heads, stacked = [], []
    for h in range(REC_HEADS):
        cols = slice(h * REC_K_DIM, (h + 1) * REC_K_DIM)
        bh, ch, fh = b_ref.at[d, h], c_ref.at[d, h], fac_ref.at[d, h]
        b = b_all[:, cols]
        c = c_all[:, cols]
        bh[...] = b
        ch[...] = c
        b_last = _sub_block_factors(bh, fh, rev, sub)
        q = q_ref[pl.ds(r0, CHUNK), cols].astype(F32)
        be = _rows_bcast(bh, ends, SUB)
        bs_parts = [jnp.zeros((SUB, LANES), F32) if r is None else jnp.broadcast_to(bh[r:r + 1, :], (SUB, LANES))
                    for r in starts]
        bs = jnp.concatenate(bs_parts, axis=0)
        q_hat = q * jnp.exp2(b - bs)
        k_hat = jnp.exp2(be - c)
        pieces = []
        for sp in range(SUB):
            crow = _rows_bcast(ch, [SUB * i + sp for i in range(NSUB)], SUB)
            pieces.append((q * jnp.exp2(jnp.minimum(b - crow, 0.0))).astype(BF16))
        stacked.append(jnp.concatenate(pieces, axis=1))
        heads.append((b_last, q_hat, k_hat))
    a_diag = jnp.dot(jnp.concatenate(stacked, axis=0), esel_ref[...], preferred_element_type=F32)
    return heads, a_diag


def _chunk_state_and_scores(r0, d, heads, a_diag, v_ref, st_ref, fac_ref, dmask, same_sub, lmasks):
    out = []
    for h, (b_last, q_hat, k_hat) in enumerate(heads):
        cols = slice(h * REC_K_DIM, (h + 1) * REC_K_DIM)
        fh = fac_ref.at[d, h]
        v = v_ref[pl.ds(r0, CHUNK), cols]
        st = st_ref[d, h]
        o = lax.dot_general((q_hat * _expand(fh, 0)).astype(BF16), st.astype(BF16), _NT,
                            preferred_element_type=F32)
        k_up = (k_hat * _expand(fh, 1)).astype(BF16)
        st_ref[d, h] = st * jnp.exp2(b_last) + lax.dot_general(v, k_up, _TN, preferred_element_type=F32)
        a = None
        for j in reversed(range(len(LEVELS))):
            q_m = (q_hat * _expand(fh, 2 + 2 * j)).astype(BF16)
            k_m = (k_hat * _expand(fh, 3 + 2 * j)).astype(BF16)
            a_m = lax.dot_general(q_m, k_m, _NT, preferred_element_type=F32)
            a = a_m if a is None else jnp.where(lmasks[j], a_m, a)
        a = jnp.where(same_sub, 0.0, a)
        a = jnp.where(dmask, a_diag[h * CHUNK:(h + 1) * CHUNK], a)
        out.append((o, a.astype(BF16), v))
    return out


def _chunk_outputs(r0, parts, acc_ref):
    for h, (o, a, v) in enumerate(parts):
        cols = slice(h * REC_V_DIM, (h + 1) * REC_V_DIM)
        acc_ref[pl.ds(r0, CHUNK), cols] = o + jnp.dot(a, v, preferred_element_type=F32)


def _hgrn_body(qf_ref, qb_ref, zf_ref, zb_ref, vf_ref, vb_ref, g_ref, lb_ref, gain_ref, esel_ref, o_ref,
               st_ref, b_ref, c_ref, fac_ref, accf_ref, accb_ref, *, S):
    nj = S // HBLOCK
    j = pl.program_id(1)

    @pl.when(j == 0)
    def _():
        st_ref[...] = jnp.zeros_like(st_ref)

    sub = lax.broadcasted_iota(jnp.int32, (NSUB, LANES), 0)
    rt = lax.broadcasted_iota(jnp.int32, (CHUNK, CHUNK), 0)
    ct = lax.broadcasted_iota(jnp.int32, (CHUNK, CHUNK), 1)
    lmasks = [_same_block(rt, ct, m) for m in LEVELS]
    same_sub = _same_block(rt, ct, SUB)
    dmasks = (same_sub & (rt >= ct), same_sub & (rt <= ct))
    tris = (jnp.where(ct <= rt, 1.0, 0.0).astype(BF16), jnp.where(ct >= rt, 1.0, 0.0).astype(BF16))
    q_refs, z_refs, v_refs = (qf_ref, qb_ref), (zf_ref, zb_ref), (vf_ref, vb_ref)
    acc_refs = (accf_ref, accb_ref)
    base = (j * HBLOCK, (nj - 1 - j) * HBLOCK)

    def step(i, carry):
        r0 = [(pl.multiple_of((HUNROLL * i + u) * CHUNK, CHUNK),
               pl.multiple_of((HSTEPS - 1 - HUNROLL * i - u) * CHUNK, CHUNK)) for u in range(HUNROLL)]
        decays = [[_chunk_decays(r0[u][d], z_refs[d], lb_ref[d:d + 1, :], tris[d]) for d in range(2)]
                  for u in range(HUNROLL)]
        ops = [[_chunk_operands(r0[u][d], d, d == 1, *decays[u][d], q_refs[d], esel_ref, b_ref.at[u], c_ref.at[u],
                                fac_ref.at[u], sub) for d in range(2)] for u in range(HUNROLL)]
        for u in range(HUNROLL):
            parts = [_chunk_state_and_scores(r0[u][d], d, *ops[u][d], v_refs[d], st_ref, fac_ref.at[u], dmasks[d],
                                             same_sub, lmasks) for d in range(2)]
            for d in range(2):
                _chunk_outputs(pl.multiple_of(base[d] + r0[u][d], CHUNK), parts[d], acc_refs[d])
        return carry

    lax.fori_loop(0, HSTEPS // HUNROLL, step, 0)

    @pl.when(j == nj - 1)
    def _():
        for h in range(REC_HEADS):
            cols = slice(h * REC_V_DIM, (h + 1) * REC_V_DIM)
            o = _rms(accf_ref[:, cols] + accb_ref[:, cols], gain_ref[...])
            g = g_ref[:, cols].astype(F32)
            o_ref[:, cols] = (o * (g * jax.nn.sigmoid(g))).astype(o_ref.dtype)


def _hgrn(q, zf, zb, v, g, lb, gain, esel, B, S):
    nj = S // HBLOCK
    fwd = lambda wd: pl.BlockSpec((HBLOCK, wd), lambda b, j: (b * nj + j, 0))
    bwd = lambda wd: pl.BlockSpec((HBLOCK, wd), lambda b, j: (b * nj + nj - 1 - j, 0))
    seq = lambda: pl.BlockSpec((S, REC_WIDTH), lambda b, j: (b, 0))
    return pl.pallas_call(
        functools.partial(_hgrn_body, S=S),
        grid=(B, nj),
        in_specs=[
            fwd(REC_WIDTH), bwd(REC_WIDTH),
            fwd(REC_WIDTH), bwd(REC_WIDTH),
            fwd(REC_WIDTH), bwd(REC_WIDTH),
            seq(),
            _resident((2, REC_WIDTH)),
            _resident((1, REC_V_DIM)),
            _resident((SUB * REC_K_DIM, CHUNK)),
        ],
        out_specs=seq(),
        out_shape=jax.ShapeDtypeStruct((B * S, REC_WIDTH), BF16),
        scratch_shapes=[
            pltpu.VMEM((2, REC_HEADS, REC_V_DIM, REC_K_DIM), F32),
            pltpu.VMEM((HUNROLL, 2, REC_HEADS, CHUNK, REC_K_DIM), F32),
            pltpu.VMEM((HUNROLL, 2, REC_HEADS, CHUNK, REC_K_DIM), F32),
            pltpu.VMEM((HUNROLL, 2, REC_HEADS, NFAC * NSUB, REC_K_DIM), F32),
            pltpu.VMEM((S, REC_WIDTH), F32),
            pltpu.VMEM((S, REC_WIDTH), F32),
        ],
        compiler_params=_params("arbitrary", "arbitrary"),
        name="hgrn2",
    )(q, q, zf, zb, v, v, g, lb, gain, esel)


FFN_TILE = 256


def _merge_ffn_body(x_ref, yp_ref, ya_ref, yr_ref, gate_ref, wup_ref, wua_ref, wur_ref, wo_ref,
                    n2_ref, wfi_ref, wfo_ref, fg_ref, o_ref, *, final):
    merged = None
    for k, (y_ref, w_ref) in enumerate(((yp_ref, wup_ref), (ya_ref, wua_ref), (yr_ref, wur_ref))):
        gate = jax.nn.sigmoid(gate_ref[:, k * D_MODEL:(k + 1) * D_MODEL].astype(F32))
        term = gate * jnp.dot(y_ref[...], w_ref[...], preferred_element_type=F32)
        merged = term if merged is None else merged + term
    x1 = x_ref[...] + jnp.dot(merged.astype(BF16), wo_ref[...], preferred_element_type=F32)
    h2 = _rms(x1, n2_ref[...]).astype(BF16)
    acc = x1
    for c in range(0, FFN_HIDDEN, FFN_TILE):
        gp = jnp.dot(h2, wfi_ref[:, c:c + FFN_TILE], preferred_element_type=F32)
        up = jnp.dot(h2, wfi_ref[:, FFN_HIDDEN + c:FFN_HIDDEN + c + FFN_TILE], preferred_element_type=F32)
        act = (gp * jax.nn.sigmoid(gp) * up).astype(BF16)
        acc = acc + jnp.dot(act, wfo_ref[c:c + FFN_TILE, :], preferred_element_type=F32)
    if final:
        acc = _rms(acc, fg_ref[...])
    o_ref[...] = acc


def _merge_ffn(x2, yp, ya, yr, gate, wup, wua, wur, wo, n2, wfi, wfo, fg, tm, final):
    T = x2.shape[0]
    row = lambda wd: pl.BlockSpec((tm, wd), lambda i: (i, 0))
    return pl.pallas_call(
        functools.partial(_merge_ffn_body, final=final),
        grid=(T // tm,),
        in_specs=[
            row(D_MODEL), row(POOL_WIDTH), row(ATT_WIDTH), row(REC_WIDTH), row(N_BRANCHES * D_MODEL),
            _resident((POOL_WIDTH, D_MODEL)), _resident((ATT_WIDTH, D_MODEL)), _resident((REC_WIDTH, D_MODEL)),
            _resident((D_MODEL, D_MODEL)), _resident((1, D_MODEL)),
            _resident((D_MODEL, 2 * FFN_HIDDEN)), _resident((FFN_HIDDEN, D_MODEL)), _resident((1, D_MODEL)),
        ],
        out_specs=row(D_MODEL),
        out_shape=jax.ShapeDtypeStruct((T, D_MODEL), F32),
        compiler_params=_params("arbitrary"),
        name="merge_ffn",
    )(x2, yp, ya, yr, gate, wup, wua, wur, wo, n2, wfi, wfo, fg)


def _diag_select():
    j = lax.broadcasted_iota(jnp.int32, (SUB * REC_K_DIM, CHUNK), 0) // REC_K_DIM
    c = lax.broadcasted_iota(jnp.int32, (SUB * REC_K_DIM, CHUNK), 1)
    return (c % SUB == j).astype(BF16)


@jax.jit
def kernel(x, norm1_g, w_in, pool_w, pool_scale, lam_q1, lam_k1, lam_q2, lam_k2, diff_norm_g, hgrn_lb,
           hgrn_norm_g, w_up_pool, w_up_attn, w_up_rec, w_out, norm2_g, w_ffn_in, w_ffn_out, final_norm_g):
    B, S, D = x.shape
    assert D == D_MODEL and w_in.shape[-1] == IN_COLS and S % HBLOCK == 0
    T = B * S
    tm = min(512, T)
    qb = min(256, S)
    x2 = x.reshape(T, D).astype(F32)

    lb_all = jnp.cumsum(jax.nn.softmax(hgrn_lb.astype(F32), axis=1), axis=1)
    lb_all = lb_all - lb_all[:, :1]
    esel = _diag_select()
    row = lambda a: a.reshape(1, -1).astype(F32)

    for l in range(DEPTH):
        lambda_init = 0.8 - 0.6 * math.exp(-0.3 * l)
        (u_pool, aq, ak, av, rq, rf, rb, ri, rg, gate) = _inproj(x2, row(norm1_g[l]), w_in[l].astype(BF16), tm)
        y_pool = _pool(u_pool, pool_w[l].astype(BF16), row(pool_scale[l]), B, S)
        y_attn = _attn(aq, ak, av, row(lam_q1[l]), row(lam_k1[l]), row(lam_q2[l]), row(lam_k2[l]),
                       row(diff_norm_g[l]), B, S, qb, lambda_init)
        y_rec = _hgrn(rq, rf, rb, ri, rg, lb_all[:, l], row(hgrn_norm_g[l]), esel, B, S)
        x2 = _merge_ffn(x2, y_pool, y_attn, y_rec, gate,
                        w_up_pool[l].astype(BF16), w_up_attn[l].astype(BF16), w_up_rec[l].astype(BF16),
                        w_out[l].astype(BF16), row(norm2_g[l]), w_ffn_in[l].astype(BF16),
                        w_ffn_out[l].astype(BF16), row(final_norm_g), tm, l == DEPTH - 1)
    return x2.reshape(B, S, D).astype(x.dtype)
```

```python
import functools
import math

import jax
import jax.numpy as jnp
from jax import lax
from jax.experimental import pallas as pl
from jax.experimental.pallas import tpu as pltpu

F32 = jnp.float32
BF16 = jnp.bfloat16

D_MODEL = 1024
DEPTH = 4
N_POOL_GROUPS = 4
POOL_GROUP_DIM = 128
POOL_WIDTH = N_POOL_GROUPS * POOL_GROUP_DIM
POOL_WINDOWS = (2, 4, 8, 16)
ATT_HEADS = 4
ATT_QK_DIM = 64
ATT_V_DIM = 2 * ATT_QK_DIM
ATT_WIDTH = ATT_HEADS * ATT_V_DIM
REC_HEADS = 4
REC_K_DIM = 128
REC_V_DIM = 128
REC_WIDTH = REC_HEADS * REC_V_DIM
N_BRANCHES = 3
FFN_HIDDEN = ((math.ceil(8 * D_MODEL / 3) + 255) // 256) * 256
NORM_EPS = 1e-6
LOG2E = math.log2(math.e)

SUBLANES = 8
LANES = 128
VMEM_LIMIT_BYTES = 56 * 1024 * 1024

def _alibi_slope(h):
    return 2.0 ** (-8.0 * (h + 1) / ATT_HEADS)


IN_SECTIONS = (
    ("pool", POOL_WIDTH, 1.0),
    ("aq", ATT_WIDTH, None),
    ("ak", ATT_WIDTH, 1.0),
    ("av", ATT_WIDTH, 1.0),
    ("rq", REC_WIDTH, REC_K_DIM ** -0.5),
    ("rf", REC_WIDTH, 1.0),
    ("rb", REC_WIDTH, 1.0),
    ("ri", REC_WIDTH, 1.0),
    ("rg", REC_WIDTH, 1.0),
    ("gate", N_BRANCHES * D_MODEL, 1.0),
)
IN_COLS = sum(s[1] for s in IN_SECTIONS)
IN_OUTPUTS = (
    ("pool", POOL_WIDTH, BF16), ("aq", ATT_WIDTH, BF16), ("ak", ATT_WIDTH, BF16), ("av", 2 * ATT_WIDTH, BF16),
    ("rq", REC_WIDTH, BF16), ("rf", REC_WIDTH, F32), ("rb", REC_WIDTH, F32),
    ("ri", REC_WIDTH, BF16), ("rg", REC_WIDTH, BF16), ("gate", N_BRANCHES * D_MODEL, BF16),
)

CHUNK = 128
SUB = SUBLANES
NSUB = CHUNK // SUB
LEVELS = tuple(2 * SUB * 2 ** i for i in range(int(math.log2(CHUNK // (2 * SUB))) + 1))
NFAC = 2 + 2 * len(LEVELS)
NEG_BIG = -1e30


def _params(*sem):
    return pltpu.CompilerParams(dimension_semantics=sem, vmem_limit_bytes=VMEM_LIMIT_BYTES)


def _resident(shape):
    return pl.BlockSpec(shape, lambda *_: (0,) * len(shape), pipeline_mode=pl.Buffered(1))


def _rms(x, g):
    return x * lax.rsqrt(jnp.mean(x * x, axis=-1, keepdims=True) + NORM_EPS) * g


def _log_gates(z, lb):
    a = jnp.log(lb)
    l1m = jnp.log1p(-lb)
    soft = jnp.log(1.0 + jnp.exp(-jnp.abs(z)))
    c = l1m + (jnp.minimum(z, 0.0) - soft)
    logf = jnp.maximum(a, c) + jnp.log(1.0 + jnp.exp(-jnp.abs(a - c)))
    logk = l1m - jnp.maximum(z, 0.0) - soft
    return logf * LOG2E, logk * LOG2E


def _inproj_body(x_ref, g_ref, w_ref, *refs):
    out = {name: ref for ref, (name, _, _) in zip(refs, IN_OUTPUTS)}
    h = _rms(x_ref[...], g_ref[...]).astype(BF16)
    off = 0
    for name, width, scale in IN_SECTIONS:
        for c in range(0, width, 512):
            y = jnp.dot(h, w_ref[:, off + c:off + c + 512], preferred_element_type=F32)
            if name == "aq":
                col = lax.broadcasted_iota(jnp.int32, (1, ATT_WIDTH), 1)
                row_scale = jnp.full((1, ATT_WIDTH), ATT_QK_DIM ** -0.5 / _alibi_slope(ATT_HEADS - 1), F32)
                for hd in range(ATT_HEADS - 2, -1, -1):
                    row_scale = jnp.where(col < (hd + 1) * ATT_V_DIM, ATT_QK_DIM ** -0.5 / _alibi_slope(hd), row_scale)
                y = y * row_scale
            elif scale != 1.0:
                y = y * scale
            if name == "av":
                o_ref = out["av"]
                for hd in range(ATT_HEADS):
                    o_ref[:, 2 * hd * ATT_V_DIM:(2 * hd + 1) * ATT_V_DIM] = (
                        y[:, hd * ATT_V_DIM:(hd + 1) * ATT_V_DIM].astype(o_ref.dtype))
                    o_ref[:, (2 * hd + 1) * ATT_V_DIM:(2 * hd + 2) * ATT_V_DIM] = jnp.ones(
                        (y.shape[0], ATT_V_DIM), o_ref.dtype)
            else:
                out[name][:, c:c + 512] = y.astype(out[name].dtype)
        off += width


def _inproj(x2, g, w, tm):
    T = x2.shape[0]
    return pl.pallas_call(
        _inproj_body,
        grid=(T // tm,),
        in_specs=[
            pl.BlockSpec((tm, D_MODEL), lambda i: (i, 0)),
            _resident((1, D_MODEL)),
            _resident((D_MODEL, IN_COLS)),
        ],
        out_specs=[pl.BlockSpec((tm, wd), lambda i: (i, 0)) for _, wd, _ in IN_OUTPUTS],
        out_shape=[jax.ShapeDtypeStruct((T, wd), dt) for _, wd, dt in IN_OUTPUTS],
        compiler_params=_params("arbitrary"),
        name="inproj",
    )(x2, g, w)


POOL_PAD = SUBLANES


def _pool_body(u_ref, w_ref, sc_ref, icnt_ref, o_ref, *, S):
    n = S + 2 * POOL_PAD
    zpad = jnp.zeros((POOL_PAD, POOL_GROUP_DIM), F32)
    for g, win in enumerate(POOL_WINDOWS):
        cols = slice(g * POOL_GROUP_DIM, (g + 1) * POOL_GROUP_DIM)
        u = u_ref[:, cols].astype(F32)
        p = jnp.concatenate([zpad, u, zpad], axis=0)
        sh = 1
        while sh < win:
            p = p + pltpu.roll(p, sh, axis=0)
            sh *= 2
        ahead = win // 2 - 1
        if ahead:
            p = pltpu.roll(p, n - ahead, axis=0)
        tot = p[POOL_PAD:POOL_PAD + S]
        d = (tot * icnt_ref[g] - u).astype(BF16)
        y = jnp.dot(d, w_ref[g], preferred_element_type=F32) * sc_ref[:, cols]
        o_ref[:, cols] = y.astype(o_ref.dtype)


def _pool(u, w, sc, B, S):
    t = jnp.arange(S)
    icnt = jnp.stack([1.0 / (jnp.minimum(t + win // 2 - 1, S - 1) - jnp.maximum(t - win // 2, 0) + 1).astype(F32)
                      for win in POOL_WINDOWS])
    icnt = jnp.broadcast_to(icnt[:, :, None], (N_POOL_GROUPS, S, POOL_GROUP_DIM))
    return pl.pallas_call(
        functools.partial(_pool_body, S=S),
        grid=(B,),
        in_specs=[
            pl.BlockSpec((S, POOL_WIDTH), lambda b: (b, 0)),
            _resident((N_POOL_GROUPS, POOL_GROUP_DIM, POOL_GROUP_DIM)),
            _resident((1, POOL_WIDTH)),
            _resident((N_POOL_GROUPS, S, POOL_GROUP_DIM)),
        ],
        out_specs=pl.BlockSpec((S, POOL_WIDTH), lambda b: (b, 0)),
        out_shape=jax.ShapeDtypeStruct((B * S, POOL_WIDTH), BF16),
        compiler_params=_params("arbitrary"),
        name="pool",
    )(u, w, sc, icnt)


_NT = (((1,), (1,)), ((), ()))
_TN = (((0,), (0,)), ((), ()))


def _attn_body(q_ref, k_ref, v_ref, dist_ref, lq1_ref, lk1_ref, lq2_ref, lk2_ref, gain_ref, o_ref, s_ref,
               *, S, qb, lambda_init):
    qi = pl.program_id(1)
    lam = (jnp.exp(jnp.sum(lq1_ref[...] * lk1_ref[...], axis=-1, keepdims=True))
           - jnp.exp(jnp.sum(lq2_ref[...] * lk2_ref[...], axis=-1, keepdims=True))
           + lambda_init)
    dist = dist_ref.at[:, pl.ds(pl.multiple_of((S // qb - 1 - qi) * qb, qb), S)]
    lane = lax.broadcasted_iota(jnp.int32, (qb, ATT_V_DIM), 1)

    for h in range(ATT_HEADS):
        cols = slice(h * ATT_V_DIM, (h + 1) * ATT_V_DIM)
        qh = q_ref[:, cols]
        zero = jnp.zeros_like(qh)
        qs = jnp.concatenate([jnp.where(lane < ATT_QK_DIM, qh, zero),
                              jnp.where(lane >= ATT_QK_DIM, qh, zero)], axis=0)
        s_ref[h] = lax.dot_general(qs, k_ref[:, cols], _NT, preferred_element_type=F32)
    for h in range(ATT_HEADS):
        cols = slice(h * ATT_V_DIM, (h + 1) * ATT_V_DIM)
        v_ext = v_ref[:, 2 * h * ATT_V_DIM:(2 * h + 2) * ATT_V_DIM]
        slope2 = _alibi_slope(h) * LOG2E
        sm = []
        for c in range(2):
            rows = slice(c * qb, (c + 1) * qb)
            m = jnp.max(s_ref[h, rows, :] - dist[...], axis=-1, keepdims=True)
            p = jnp.exp2(((s_ref[h, rows, :] - m) - dist[...]) * slope2).astype(BF16)
            ol = jnp.dot(p, v_ext, preferred_element_type=F32)
            sm.append(ol[:, :ATT_V_DIM] / ol[:, ATT_V_DIM:])
        o = _rms(sm[0] - lam * sm[1], gain_ref[...]) * (1.0 - lambda_init)
        o_ref[:, cols] = o.astype(o_ref.dtype)


def _attn(q, k, v, lq1, lk1, lq2, lk2, gain, B, S, qb, lambda_init):
    small = _resident((1, ATT_QK_DIM))
    width = 2 * S - qb
    table = jnp.abs(jnp.arange(qb)[:, None] + (S - qb) - jnp.arange(width)[None, :]).astype(F32)
    return pl.pallas_call(
        functools.partial(_attn_body, S=S, qb=qb, lambda_init=lambda_init),
        grid=(B, S // qb),
        in_specs=[
            pl.BlockSpec((qb, ATT_WIDTH), lambda b, i: (b * (S // qb) + i, 0)),
            pl.BlockSpec((S, ATT_WIDTH), lambda b, i: (b, 0)),
            pl.BlockSpec((S, 2 * ATT_WIDTH), lambda b, i: (b, 0)),
            _resident((qb, width)),
            small, small, small, small,
            _resident((1, ATT_V_DIM)),
        ],
        out_specs=pl.BlockSpec((qb, ATT_WIDTH), lambda b, i: (b * (S // qb) + i, 0)),
        out_shape=jax.ShapeDtypeStruct((B * S, ATT_WIDTH), BF16),
        scratch_shapes=[pltpu.VMEM((ATT_HEADS, 2 * qb, S), F32)],
        compiler_params=_params("arbitrary", "arbitrary"),
        name="diffattn",
    )(q, k, v, table, lq1, lk1, lq2, lk2, gain)


HBLOCK = 1024
HSTEPS = HBLOCK // CHUNK
HUNROLL = 2


def _same_block(r, c, size):
    return ((r ^ c) & ~(size - 1)) == 0


def _rows_bcast(ref, rows, n):
    parts = [jnp.broadcast_to(ref[r:r + 1, :], (n, LANES)) for r in rows]
    return parts[0] if len(parts) == 1 else jnp.concatenate(parts, axis=0)


def _sub_block_factors(b_ref, fac_ref, rev, sub):
    be = b_ref[pl.ds(0 if rev else SUB - 1, NSUB, stride=SUB), :]
    if rev:
        bs = jnp.where(sub == NSUB - 1, 0.0, pltpu.roll(be, NSUB - 1, axis=0))
        last = 0
    else:
        bs = jnp.where(sub == 0, 0.0, pltpu.roll(be, 1, axis=0))
        last = CHUNK - 1
    b_last = b_ref[last:last + 1, :]
    fac_ref[0:NSUB, :] = jnp.exp2(bs)
    fac_ref[NSUB:2 * NSUB, :] = jnp.exp2(b_last - be)
    for j, m in enumerate(LEVELS):
        per = m // SUB
        mids = [m * blk + (m // 2 if rev else m // 2 - 1) for blk in range(CHUNK // m)]
        bmid = jnp.broadcast_to(b_ref[mids[-1]:mids[-1] + 1, :], (NSUB, LANES))
        for blk in range(CHUNK // m - 2, -1, -1):
            row = jnp.broadcast_to(b_ref[mids[blk]:mids[blk] + 1, :], (NSUB, LANES))
            bmid = jnp.where(sub < (blk + 1) * per, row, bmid)
        pos = sub & (per - 1)
        late = (pos < per // 2) if rev else (pos >= per // 2)
        fac_ref[(2 + 2 * j) * NSUB:(3 + 2 * j) * NSUB, :] = jnp.exp2(jnp.where(late, bs - bmid, NEG_BIG))
        fac_ref[(3 + 2 * j) * NSUB:(4 + 2 * j) * NSUB, :] = jnp.exp2(jnp.where(late, NEG_BIG, bmid - be))
    return b_last


def _expand(fac_ref, f):
    return _rows_bcast(fac_ref, [f * NSUB + i for i in range(NSUB)], SUB)


def _chunk_decays(r0, z_ref, lb, tri):
    logf, logk = _log_gates(z_ref[pl.ds(r0, CHUNK), :], lb)
    hi = logf.astype(BF16)
    lo = (logf - hi.astype(F32)).astype(BF16)
    cs = jnp.dot(tri, jnp.concatenate([hi, lo], axis=1), preferred_element_type=F32)
    b_all = cs[:, :REC_WIDTH] + cs[:, REC_WIDTH:]
    return b_all, b_all - logk


def _chunk_operands(r0, d, rev, b_all, c_all, q_ref, esel_ref, b_ref, c_ref, fac_ref, sub):
    starts = ([SUB * (i + 1) for i in range(NSUB - 1)] + [None]) if rev else ([None] + [SUB * i - 1 for i in range(1, NSUB)])
    ends = [SUB * i for i in range(NSUB)] if rev else [SUB * i + SUB - 1 for i in range(NSUB)]
    heads, stacked = [], []
    for h in range(REC_HEADS):
        cols = slice(h * REC_K_DIM, (h + 1) * REC_K_DIM)
        bh, ch, fh = b_ref.at[d, h], c_ref.at[d, h], fac_ref.at[d, h]
        b = b_all[:, cols]
        c = c_all[:, cols]
        bh[...] = b
        ch[...] = c
        b_last = _sub_block_factors(bh, fh, rev, sub)
        q = q_ref[pl.ds(r0, CHUNK), cols].astype(F32)
        be = _rows_bcast(bh, ends, SUB)
        bs_parts = [jnp.zeros((SUB, LANES), F32) if r is None else jnp.broadcast_to(bh[r:r + 1, :], (SUB, LANES))
                    for r in starts]
        bs = jnp.concatenate(bs_parts, axis=0)
        q_hat = q * jnp.exp2(b - bs)
        k_hat = jnp.exp2(be - c)
        pieces = []
        for sp in range(SUB):
            crow = _rows_bcast(ch, [SUB * i + sp for i in range(NSUB)], SUB)
            pieces.append((q * jnp.exp2(jnp.minimum(b - crow, 0.0))).astype(BF16))
        stacked.append(jnp.concatenate(pieces, axis=1))
        heads.append((b_last, q_hat, k_hat))
    a_diag = jnp.dot(jnp.concatenate(stacked, axis=0), esel_ref[...], preferred_element_type=F32)
    return heads, a_diag


def _chunk_state_and_scores(r0, d, heads, a_diag, v_ref, st_ref, fac_ref, dmask, same_sub, lmasks):
    out = []
    for h, (b_last, q_hat, k_hat) in enumerate(heads):
        cols = slice(h * REC_K_DIM, (h + 1) * REC_K_DIM)
        fh = fac_ref.at[d, h]
        v = v_ref[pl.ds(r0, CHUNK), cols]
        st = st_ref[d, h]
        o = lax.dot_general((q_hat * _expand(fh, 0)).astype(BF16), st.astype(BF16), _NT,
                            preferred_element_type=F32)
        k_up = (k_hat * _expand(fh, 1)).astype(BF16)
        st_ref[d, h] = st * jnp.exp2(b_last) + lax.dot_general(v, k_up, _TN, preferred_element_type=F32)
        a = None
        for j in reversed(range(len(LEVELS))):
            q_m = (q_hat * _expand(fh, 2 + 2 * j)).astype(BF16)
            k_m = (k_hat * _expand(fh, 3 + 2 * j)).astype(BF16)
            a_m = lax.dot_general(q_m, k_m, _NT, preferred_element_type=F32)
            a = a_m if a is None else jnp.where(lmasks[j], a_m, a)
        a = jnp.where(same_sub, 0.0, a)
        a = jnp.where(dmask, a_diag[h * CHUNK:(h + 1) * CHUNK], a)
        out.append((o, a.astype(BF16), v))
    return out


def _chunk_outputs(r0, parts, acc_ref):
    for h, (o, a, v) in enumerate(parts):
        cols = slice(h * REC_V_DIM, (h + 1) * REC_V_DIM)
        acc_ref[pl.ds(r0, CHUNK), cols] = o + jnp.dot(a, v, preferred_element_type=F32)


def _hgrn_body(qf_ref, qb_ref, zf_ref, zb_ref, vf_ref, vb_ref, g_ref, lb_ref, gain_ref, esel_ref, o_ref,
               st_ref, b_ref, c_ref, fac_ref, accf_ref, accb_ref, *, S):
    nj = S // HBLOCK
    j = pl.program_id(1)

    @pl.when(j == 0)
    def _():
        st_ref[...] = jnp.zeros_like(st_ref)

    sub = lax.broadcasted_iota(jnp.int32, (NSUB, LANES), 0)
    rt = lax.broadcasted_iota(jnp.int32, (CHUNK, CHUNK), 0)
    ct = lax.broadcasted_iota(jnp.int32, (CHUNK, CHUNK), 1)
    lmasks = [_same_block(rt, ct, m) for m in LEVELS]
    same_sub = _same_block(rt, ct, SUB)
    dmasks = (same_sub & (rt >= ct), same_sub & (rt <= ct))
    tris = (jnp.where(ct <= rt, 1.0, 0.0).astype(BF16), jnp.where(ct >= rt, 1.0, 0.0).astype(BF16))
    q_refs, z_refs, v_refs = (qf_ref, qb_ref), (zf_ref, zb_ref), (vf_ref, vb_ref)
    acc_refs = (accf_ref, accb_ref)
    base = (j * HBLOCK, (nj - 1 - j) * HBLOCK)

    def step(i, carry):
        r0 = [(pl.multiple_of((HUNROLL * i + u) * CHUNK, CHUNK),
               pl.multiple_of((HSTEPS - 1 - HUNROLL * i - u) * CHUNK, CHUNK)) for u in range(HUNROLL)]
        decays = [[_chunk_decays(r0[u][d], z_refs[d], lb_ref[d:d + 1, :], tris[d]) for d in range(2)]
                  for u in range(HUNROLL)]
        ops = [[_chunk_operands(r0[u][d], d, d == 1, *decays[u][d], q_refs[d], esel_ref, b_ref.at[u], c_ref.at[u],
                                fac_ref.at[u], sub) for d in range(2)] for u in range(HUNROLL)]
        for u in range(HUNROLL):
            parts = [_chunk_state_and_scores(r0[u][d], d, *ops[u][d], v_refs[d], st_ref, fac_ref.at[u], dmasks[d],
                                             same_sub, lmasks) for d in range(2)]
            for d in range(2):
                _chunk_outputs(pl.multiple_of(base[d] + r0[u][d], CHUNK), parts[d], acc_refs[d])
        return carry

    lax.fori_loop(0, HSTEPS // HUNROLL, step, 0)

    @pl.when(j == nj - 1)
    def _():
        for h in range(REC_HEADS):
            cols = slice(h * REC_V_DIM, (h + 1) * REC_V_DIM)
            o = _rms(accf_ref[:, cols] + accb_ref[:, cols], gain_ref[...])
            g = g_ref[:, cols].astype(F32)
            o_ref[:, cols] = (o * (g * jax.nn.sigmoid(g))).astype(o_ref.dtype)


def _hgrn(q, zf, zb, v, g, lb, gain, esel, B, S):
    nj = S // HBLOCK
    fwd = lambda wd: pl.BlockSpec((HBLOCK, wd), lambda b, j: (b * nj + j, 0))
    bwd = lambda wd: pl.BlockSpec((HBLOCK, wd), lambda b, j: (b * nj + nj - 1 - j, 0))
    seq = lambda: pl.BlockSpec((S, REC_WIDTH), lambda b, j: (b, 0))
    return pl.pallas_call(
        functools.partial(_hgrn_body, S=S),
        grid=(B, nj),
        in_specs=[
            fwd(REC_WIDTH), bwd(REC_WIDTH),
            fwd(REC_WIDTH), bwd(REC_WIDTH),
            fwd(REC_WIDTH), bwd(REC_WIDTH),
            seq(),
            _resident((2, REC_WIDTH)),
            _resident((1, REC_V_DIM)),
            _resident((SUB * REC_K_DIM, CHUNK)),
        ],
        out_specs=seq(),
        out_shape=jax.ShapeDtypeStruct((B * S, REC_WIDTH), BF16),
        scratch_shapes=[
            pltpu.VMEM((2, REC_HEADS, REC_V_DIM, REC_K_DIM), F32),
            pltpu.VMEM((HUNROLL, 2, REC_HEADS, CHUNK, REC_K_DIM), F32),
            pltpu.VMEM((HUNROLL, 2, REC_HEADS, CHUNK, REC_K_DIM), F32),
            pltpu.VMEM((HUNROLL, 2, REC_HEADS, NFAC * NSUB, REC_K_DIM), F32),
            pltpu.VMEM((S, REC_WIDTH), F32),
            pltpu.VMEM((S, REC_WIDTH), F32),
        ],
        compiler_params=_params("arbitrary", "arbitrary"),
        name="hgrn2",
    )(q, q, zf, zb, v, v, g, lb, gain, esel)


FFN_TILE = 256


def _merge_ffn_body(x_ref, yp_ref, ya_ref, yr_ref, gate_ref, wup_ref, wua_ref, wur_ref, wo_ref,
                    n2_ref, wfi_ref, wfo_ref, fg_ref, o_ref, *, final):
    merged = None
    for k, (y_ref, w_ref) in enumerate(((yp_ref, wup_ref), (ya_ref, wua_ref), (yr_ref, wur_ref))):
        gate = jax.nn.sigmoid(gate_ref[:, k * D_MODEL:(k + 1) * D_MODEL].astype(F32))
        term = gate * jnp.dot(y_ref[...], w_ref[...], preferred_element_type=F32)
        merged = term if merged is None else merged + term
    x1 = x_ref[...] + jnp.dot(merged.astype(BF16), wo_ref[...], preferred_element_type=F32)
    h2 = _rms(x1, n2_ref[...]).astype(BF16)
    acc = x1
    for c in range(0, FFN_HIDDEN, FFN_TILE):
        gp = jnp.dot(h2, wfi_ref[:, c:c + FFN_TILE], preferred_element_type=F32)
        up = jnp.dot(h2, wfi_ref[:, FFN_HIDDEN + c:FFN_HIDDEN + c + FFN_TILE], preferred_element_type=F32)
        act = (gp * jax.nn.sigmoid(gp) * up).astype(BF16)
        acc = acc + jnp.dot(act, wfo_ref[c:c + FFN_TILE, :], preferred_element_type=F32)
    if final:
        acc = _rms(acc, fg_ref[...])
    o_ref[...] = acc


def _merge_ffn(x2, yp, ya, yr, gate, wup, wua, wur, wo, n2, wfi, wfo, fg, tm, final):
    T = x2.shape[0]
    row = lambda wd: pl.BlockSpec((tm, wd), lambda i: (i, 0))
    return pl.pallas_call(
        functools.partial(_merge_ffn_body, final=final),
        grid=(T // tm,),
        in_specs=[
            row(D_MODEL), row(POOL_WIDTH), row(ATT_WIDTH), row(REC_WIDTH), row(N_BRANCHES * D_MODEL),
            _resident((POOL_WIDTH, D_MODEL)), _resident((ATT_WIDTH, D_MODEL)), _resident((REC_WIDTH, D_MODEL)),
            _resident((D_MODEL, D_MODEL)), _resident((1, D_MODEL)),
            _resident((D_MODEL, 2 * FFN_HIDDEN)), _resident((FFN_HIDDEN, D_MODEL)), _resident((1, D_MODEL)),
        ],
        out_specs=row(D_MODEL),
        out_shape=jax.ShapeDtypeStruct((T, D_MODEL), F32),
        compiler_params=_params("arbitrary"),
        name="merge_ffn",
    )(x2, yp, ya, yr, gate, wup, wua, wur, wo, n2, wfi, wfo, fg)


def _diag_select():
    j = lax.broadcasted_iota(jnp.int32, (SUB * REC_K_DIM, CHUNK), 0) // REC_K_DIM
    c = lax.broadcasted_iota(jnp.int32, (SUB * REC_K_DIM, CHUNK), 1)
    return (c % SUB == j).astype(BF16)


@jax.jit
def kernel(x, norm1_g, w_in, pool_w, pool_scale, lam_q1, lam_k1, lam_q2, lam_k2, diff_norm_g, hgrn_lb,
           hgrn_norm_g, w_up_pool, w_up_attn, w_up_rec, w_out, norm2_g, w_ffn_in, w_ffn_out, final_norm_g):
    B, S, D = x.shape
    assert D == D_MODEL and w_in.shape[-1] == IN_COLS and S % HBLOCK == 0
    T = B * S
    tm = min(512, T)
    qb = min(256, S)
    x2 = x.reshape(T, D).astype(F32)

    lb_all = jnp.cumsum(jax.nn.softmax(hgrn_lb.astype(F32), axis=1), axis=1)
    lb_all = lb_all - lb_all[:, :1]
    esel = _diag_select()
    row = lambda a: a.reshape(1, -1).astype(F32)

    for l in range(DEPTH):
        lambda_init = 0.8 - 0.6 * math.exp(-0.3 * l)
        (u_pool, aq, ak, av, rq, rf, rb, ri, rg, gate) = _inproj(x2, row(norm1_g[l]), w_in[l].astype(BF16), tm)
        y_pool = _pool(u_pool, pool_w[l].astype(BF16), row(pool_scale[l]), B, S)
        y_attn = _attn(aq, ak, av, row(lam_q1[l]), row(lam_k1[l]), row(lam_q2[l]), row(lam_k2[l]),
                       row(diff_norm_g[l]), B, S, qb, lambda_init)
        y_rec = _hgrn(rq, rf, rb, ri, rg, lb_all[:, l], row(hgrn_norm_g[l]), esel, B, S)
        x2 = _merge_ffn(x2, y_pool, y_attn, y_rec, gate,
                        w_up_pool[l].astype(BF16), w_up_attn[l].astype(BF16), w_up_rec[l].astype(BF16),
                        w_out[l].astype(BF16), row(norm2_g[l]), w_ffn_in[l].astype(BF16),
                        w_ffn_out[l].astype(BF16), row(final_norm_g), tm, l == DEPTH - 1)
    return x2.reshape(B, S, D).astype(x.dtype)
```

```python
import functools
import math

import jax
import jax.numpy as jnp
from jax import lax
from jax.experimental import pallas as pl
from jax.experimental.pallas import tpu as pltpu

F32 = jnp.float32
BF16 = jnp.bfloat16

D_MODEL = 1024
DEPTH = 4
N_POOL_GROUPS = 4
POOL_GROUP_DIM = 128
POOL_WIDTH = N_POOL_GROUPS * POOL_GROUP_DIM
POOL_WINDOWS = (2, 4, 8, 16)
ATT_HEADS = 4
ATT_QK_DIM = 64
ATT_V_DIM = 2 * ATT_QK_DIM
ATT_WIDTH = ATT_HEADS * ATT_V_DIM
REC_HEADS = 4
REC_K_DIM = 128
REC_V_DIM = 128
REC_WIDTH = REC_HEADS * REC_V_DIM
N_BRANCHES = 3
FFN_HIDDEN = ((math.ceil(8 * D_MODEL / 3) + 255) // 256) * 256
NORM_EPS = 1e-6
LOG2E = math.log2(math.e)

SUBLANES = 8
LANES = 128
VMEM_LIMIT_BYTES = 56 * 1024 * 1024

def _alibi_slope(h):
    return 2.0 ** (-8.0 * (h + 1) / ATT_HEADS)


IN_SECTIONS = (
    ("pool", POOL_WIDTH, 1.0),
    ("aq", ATT_WIDTH, None),
    ("ak", ATT_WIDTH, 1.0),
    ("av", ATT_WIDTH, 1.0),
    ("rq", REC_WIDTH, REC_K_DIM ** -0.5),
    ("rf", REC_WIDTH, 1.0),
    ("rb", REC_WIDTH, 1.0),
    ("ri", REC_WIDTH, 1.0),
    ("rg", REC_WIDTH, 1.0),
    ("gate", N_BRANCHES * D_MODEL, 1.0),
)
IN_COLS = sum(s[1] for s in IN_SECTIONS)
IN_OUTPUTS = (
    ("pool", POOL_WIDTH, BF16), ("aq", ATT_WIDTH, BF16), ("ak", ATT_WIDTH, BF16), ("av", 2 * ATT_WIDTH, BF16),
    ("rq", REC_WIDTH, BF16), ("rf", REC_WIDTH, F32), ("rb", REC_WIDTH, F32),
    ("ri", REC_WIDTH, BF16), ("rg", REC_WIDTH, BF16), ("gate", N_BRANCHES * D_MODEL, BF16),
)

CHUNK = 128
SUB = SUBLANES
NSUB = CHUNK // SUB
LEVELS = tuple(2 * SUB * 2 ** i for i in range(int(math.log2(CHUNK // (2 * SUB))) + 1))
NFAC = 2 + 2 * len(LEVELS)
NEG_BIG = -1e30


def _params(*sem):
    return pltpu.CompilerParams(dimension_semantics=sem, vmem_limit_bytes=VMEM_LIMIT_BYTES)


def _resident(shape):
    return pl.BlockSpec(shape, lambda *_: (0,) * len(shape), pipeline_mode=pl.Buffered(1))


def _rms(x, g):
    return x * lax.rsqrt(jnp.mean(x * x, axis=-1, keepdims=True) + NORM_EPS) * g


def _log_gates(z, lb):
    a = jnp.log(lb)
    l1m = jnp.log1p(-lb)
    soft = jnp.log(1.0 + jnp.exp(-jnp.abs(z)))
    c = l1m + (jnp.minimum(z, 0.0) - soft)
    logf = jnp.maximum(a, c) + jnp.log(1.0 + jnp.exp(-jnp.abs(a - c)))
    logk = l1m - jnp.maximum(z, 0.0) - soft
    return logf * LOG2E, logk * LOG2E


def _inproj_body(x_ref, g_ref, w_ref, *refs):
    out = {name: ref for ref, (name, _, _) in zip(refs, IN_OUTPUTS)}
    h = _rms(x_ref[...], g_ref[...]).astype(BF16)
    off = 0
    for name, width, scale in IN_SECTIONS:
        for c in range(0, width, 512):
            y = jnp.dot(h, w_ref[:, off + c:off + c + 512], preferred_element_type=F32)
            if name == "aq":
                col = lax.broadcasted_iota(jnp.int32, (1, ATT_WIDTH), 1)
                row_scale = jnp.full((1, ATT_WIDTH), ATT_QK_DIM ** -0.5 / _alibi_slope(ATT_HEADS - 1), F32)
                for hd in range(ATT_HEADS - 2, -1, -1):
                    row_scale = jnp.where(col < (hd + 1) * ATT_V_DIM, ATT_QK_DIM ** -0.5 / _alibi_slope(hd), row_scale)
                y = y * row_scale
            elif scale != 1.0:
                y = y * scale
            if name == "av":
                o_ref = out["av"]
                for hd in range(ATT_HEADS):
                    o_ref[:, 2 * hd * ATT_V_DIM:(2 * hd + 1) * ATT_V_DIM] = (
                        y[:, hd * ATT_V_DIM:(hd + 1) * ATT_V_DIM].astype(o_ref.dtype))
                    o_ref[:, (2 * hd + 1) * ATT_V_DIM:(2 * hd + 2) * ATT_V_DIM] = jnp.ones(
                        (y.shape[0], ATT_V_DIM), o_ref.dtype)
            else:
                out[name][:, c:c + 512] = y.astype(out[name].dtype)
        off += width


def _inproj(x2, g, w, tm):
    T = x2.shape[0]
    return pl.pallas_call(
        _inproj_body,
        grid=(T // tm,),
        in_specs=[
            pl.BlockSpec((tm, D_MODEL), lambda i: (i, 0)),
            _resident((1, D_MODEL)),
            _resident((D_MODEL, IN_COLS)),
        ],
        out_specs=[pl.BlockSpec((tm, wd), lambda i: (i, 0)) for _, wd, _ in IN_OUTPUTS],
        out_shape=[jax.ShapeDtypeStruct((T, wd), dt) for _, wd, dt in IN_OUTPUTS],
        compiler_params=_params("arbitrary"),
        name="inproj",
    )(x2, g, w)


POOL_PAD = SUBLANES


def _pool_body(u_ref, w_ref, sc_ref, icnt_ref, o_ref, *, S):
    n = S + 2 * POOL_PAD
    zpad = jnp.zeros((POOL_PAD, POOL_GROUP_DIM), F32)
    for g, win in enumerate(POOL_WINDOWS):
        cols = slice(g * POOL_GROUP_DIM, (g + 1) * POOL_GROUP_DIM)
        u = u_ref[:, cols].astype(F32)
        p = jnp.concatenate([zpad, u, zpad], axis=0)
        sh = 1
        while sh < win:
            p = p + pltpu.roll(p, sh, axis=0)
            sh *= 2
        ahead = win // 2 - 1
        if ahead:
            p = pltpu.roll(p, n - ahead, axis=0)
        tot = p[POOL_PAD:POOL_PAD + S]
        d = (tot * icnt_ref[g] - u).astype(BF16)
        y = jnp.dot(d, w_ref[g], preferred_element_type=F32) * sc_ref[:, cols]
        o_ref[:, cols] = y.astype(o_ref.dtype)


def _pool(u, w, sc, B, S):
    t = jnp.arange(S)
    icnt = jnp.stack([1.0 / (jnp.minimum(t + win // 2 - 1, S - 1) - jnp.maximum(t - win // 2, 0) + 1).astype(F32)
                      for win in POOL_WINDOWS])
    icnt = jnp.broadcast_to(icnt[:, :, None], (N_POOL_GROUPS, S, POOL_GROUP_DIM))
    return pl.pallas_call(
        functools.partial(_pool_body, S=S),
        grid=(B,),
        in_specs=[
            pl.BlockSpec((S, POOL_WIDTH), lambda b: (b, 0)),
            _resident((N_POOL_GROUPS, POOL_GROUP_DIM, POOL_GROUP_DIM)),
            _resident((1, POOL_WIDTH)),
            _resident((N_POOL_GROUPS, S, POOL_GROUP_DIM)),
        ],
        out_specs=pl.BlockSpec((S, POOL_WIDTH), lambda b: (b, 0)),
        out_shape=jax.ShapeDtypeStruct((B * S, POOL_WIDTH), BF16),
        compiler_params=_params("arbitrary"),
        name="pool",
    )(u, w, sc, icnt)


_NT = (((1,), (1,)), ((), ()))
_TN = (((0,), (0,)), ((), ()))


def _attn_body(q_ref, k_ref, v_ref, dist_ref, lq1_ref, lk1_ref, lq2_ref, lk2_ref, gain_ref, o_ref, s_ref,
               *, S, qb, lambda_init):
    qi = pl.program_id(1)
    lam = (jnp.exp(jnp.sum(lq1_ref[...] * lk1_ref[...], axis=-1, keepdims=True))
           - jnp.exp(jnp.sum(lq2_ref[...] * lk2_ref[...], axis=-1, keepdims=True))
           + lambda_init)
    dist = dist_ref.at[:, pl.ds(pl.multiple_of((S // qb - 1 - qi) * qb, qb), S)]
    lane = lax.broadcasted_iota(jnp.int32, (qb, ATT_V_DIM), 1)

    for h in range(ATT_HEADS):
        cols = slice(h * ATT_V_DIM, (h + 1) * ATT_V_DIM)
        qh = q_ref[:, cols]
        zero = jnp.zeros_like(qh)
        qs = jnp.concatenate([jnp.where(lane < ATT_QK_DIM, qh, zero),
                              jnp.where(lane >= ATT_QK_DIM, qh, zero)], axis=0)
        s_ref[h] = lax.dot_general(qs, k_ref[:, cols], _NT, preferred_element_type=F32)
    for h in range(ATT_HEADS):
        cols = slice(h * ATT_V_DIM, (h + 1) * ATT_V_DIM)
        v_ext = v_ref[:, 2 * h * ATT_V_DIM:(2 * h + 2) * ATT_V_DIM]
        slope2 = _alibi_slope(h) * LOG2E
        sm = []
        for c in range(2):
            rows = slice(c * qb, (c + 1) * qb)
            m = jnp.max(s_ref[h, rows, :] - dist[...], axis=-1, keepdims=True)
            p = jnp.exp2(((s_ref[h, rows, :] - m) - dist[...]) * slope2).astype(BF16)
            ol = jnp.dot(p, v_ext, preferred_element_type=F32)
            sm.append(ol[:, :ATT_V_DIM] / ol[:, ATT_V_DIM:])
        o = _rms(sm[0] - lam * sm[1], gain_ref[...]) * (1.0 - lambda_init)
        o_ref[:, cols] = o.astype(o_ref.dtype)


def _attn(q, k, v, lq1, lk1, lq2, lk2, gain, B, S, qb, lambda_init):
    small = _resident((1, ATT_QK_DIM))
    width = 2 * S - qb
    table = jnp.abs(jnp.arange(qb)[:, None] + (S - qb) - jnp.arange(width)[None, :]).astype(F32)
    return pl.pallas_call(
        functools.partial(_attn_body, S=S, qb=qb, lambda_init=lambda_init),
        grid=(B, S // qb),
        in_specs=[
            pl.BlockSpec((qb, ATT_WIDTH), lambda b, i: (b * (S // qb) + i, 0)),
            pl.BlockSpec((S, ATT_WIDTH), lambda b, i: (b, 0)),
            pl.BlockSpec((S, 2 * ATT_WIDTH), lambda b, i: (b, 0)),
            _resident((qb, width)),
            small, small, small, small,
            _resident((1, ATT_V_DIM)),
        ],
        out_specs=pl.BlockSpec((qb, ATT_WIDTH), lambda b, i: (b * (S // qb) + i, 0)),
        out_shape=jax.ShapeDtypeStruct((B * S, ATT_WIDTH), BF16),
        scratch_shapes=[pltpu.VMEM((ATT_HEADS, 2 * qb, S), F32)],
        compiler_params=_params("arbitrary", "arbitrary"),
        name="diffattn",
    )(q, k, v, table, lq1, lk1, lq2, lk2, gain)


HBLOCK = 1024
HSTEPS = HBLOCK // CHUNK
HUNROLL = 4


def _same_block(r, c, size):
    return ((r ^ c) & ~(size - 1)) == 0


def _rows_bcast(ref, rows, n):
    parts = [jnp.broadcast_to(ref[r:r + 1, :], (n, LANES)) for r in rows]
    return parts[0] if len(parts) == 1 else jnp.concatenate(parts, axis=0)


def _sub_block_factors(b_ref, fac_ref, rev, sub):
    be = b_ref[pl.ds(0 if rev else SUB - 1, NSUB, stride=SUB), :]
    if rev:
        bs = jnp.where(sub == NSUB - 1, 0.0, pltpu.roll(be, NSUB - 1, axis=0))
        last = 0
    else:
        bs = jnp.where(sub == 0, 0.0, pltpu.roll(be, 1, axis=0))
        last = CHUNK - 1
    b_last = b_ref[last:last + 1, :]
    fac_ref[0:NSUB, :] = jnp.exp2(bs)
    fac_ref[NSUB:2 * NSUB, :] = jnp.exp2(b_last - be)
    for j, m in enumerate(LEVELS):
        per = m // SUB
        mids = [m * blk + (m // 2 if rev else m // 2 - 1) for blk in range(CHUNK // m)]
        bmid = jnp.broadcast_to(b_ref[mids[-1]:mids[-1] + 1, :], (NSUB, LANES))
        for blk in range(CHUNK // m - 2, -1, -1):
            row = jnp.broadcast_to(b_ref[mids[blk]:mids[blk] + 1, :], (NSUB, LANES))
            bmid = jnp.where(sub < (blk + 1) * per, row, bmid)
        pos = sub & (per - 1)
        late = (pos < per // 2) if rev else (pos >= per // 2)
        fac_ref[(2 + 2 * j) * NSUB:(3 + 2 * j) * NSUB, :] = jnp.exp2(jnp.where(late, bs - bmid, NEG_BIG))
        fac_ref[(3 + 2 * j) * NSUB:(4 + 2 * j) * NSUB, :] = jnp.exp2(jnp.where(late, NEG_BIG, bmid - be))
    return b_last


def _expand(fac_ref, f):
    return _rows_bcast(fac_ref, [f * NSUB + i for i in range(NSUB)], SUB)


def _chunk_decays(r0, z_ref, lb, tri):
    logf, logk = _log_gates(z_ref[pl.ds(r0, CHUNK), :], lb)
    hi = logf.astype(BF16)
    lo = (logf - hi.astype(F32)).astype(BF16)
    cs = jnp.dot(tri, jnp.concatenate([hi, lo], axis=1), preferred_element_type=F32)
    b_all = cs[:, :REC_WIDTH] + cs[:, REC_WIDTH:]
    return b_all, b_all - logk


def _chunk_operands(r0, d, rev, b_all, c_all, q_ref, esel_ref, b_ref, c_ref, fac_ref, sub):
    starts = ([SUB * (i + 1) for i in range(NSUB - 1)] + [None]) if rev else ([None] + [SUB * i - 1 for i in range(1, NSUB)])
    ends = [SUB * i for i in range(NSUB)] if rev else [SUB * i + SUB - 1 for i in range(NSUB)]
    heads, stacked = [], []
    for h in range(REC_HEADS):
        cols = slice(h * REC_K_DIM, (h + 1) * REC_K_DIM)
        bh, ch, fh = b_ref.at[d, h], c_ref.at[d, h], fac_ref.at[d, h]
        b = b_all[:, cols]
        c = c_all[:, cols]
        bh[...] = b
        ch[...] = c
        b_last = _sub_block_factors(bh, fh, rev, sub)
        q = q_ref[pl.ds(r0, CHUNK), cols].astype(F32)
        be = _rows_bcast(bh, ends, SUB)
        bs_parts = [jnp.zeros((SUB, LANES), F32) if r is None else jnp.broadcast_to(bh[r:r + 1, :], (SUB, LANES))
                    for r in starts]
        bs = jnp.concatenate(bs_parts, axis=0)
        q_hat = q * jnp.exp2(b - bs)
        k_hat = jnp.exp2(be - c)
        pieces = []
        for sp in range(SUB):
            crow = _rows_bcast(ch, [SUB * i + sp for i in range(NSUB)], SUB)
            pieces.append((q * jnp.exp2(jnp.minimum(b - crow, 0.0))).astype(BF16))
        stacked.append(jnp.concatenate(pieces, axis=1))
        heads.append((b_last, q_hat, k_hat))
    a_diag = jnp.dot(jnp.concatenate(stacked, axis=0), esel_ref[...], preferred_element_type=F32)
    return heads, a_diag


def _chunk_state_and_scores(r0, d, heads, a_diag, v_ref, st_ref, fac_ref, dmask, same_sub, lmasks):
    out = []
    for h, (b_last, q_hat, k_hat) in enumerate(heads):
        cols = slice(h * REC_K_DIM, (h + 1) * REC_K_DIM)
        fh = fac_ref.at[d, h]
        v = v_ref[pl.ds(r0, CHUNK), cols]
        st = st_ref[d, h]
        o = lax.dot_general((q_hat * _expand(fh, 0)).astype(BF16), st.astype(BF16), _NT,
                            preferred_element_type=F32)
        k_up = (k_hat * _expand(fh, 1)).astype(BF16)
        st_ref[d, h] = st * jnp.exp2(b_last) + lax.dot_general(v, k_up, _TN, preferred_element_type=F32)
        a = None
        for j in reversed(range(len(LEVELS))):
            q_m = (q_hat * _expand(fh, 2 + 2 * j)).astype(BF16)
            k_m = (k_hat * _expand(fh, 3 + 2 * j)).astype(BF16)
            a_m = lax.dot_general(q_m, k_m, _NT, preferred_element_type=F32)
            a = a_m if a is None else jnp.where(lmasks[j], a_m, a)
        a = jnp.where(same_sub, 0.0, a)
        a = jnp.where(dmask, a_diag[h * CHUNK:(h + 1) * CHUNK], a)
        out.append((o, a.astype(BF16), v))
    return out


def _chunk_outputs(r0, parts, acc_ref):
    for h, (o, a, v) in enumerate(parts):
        cols = slice(h * REC_V_DIM, (h + 1) * REC_V_DIM)
        acc_ref[pl.ds(r0, CHUNK), cols] = o + jnp.dot(a, v, preferred_element_type=F32)


def _hgrn_body(qf_ref, qb_ref, zf_ref, zb_ref, vf_ref, vb_ref, g_ref, lb_ref, gain_ref, esel_ref, o_ref,
               st_ref, b_ref, c_ref, fac_ref, accf_ref, accb_ref, *, S):
    nj = S // HBLOCK
    j = pl.program_id(1)

    @pl.when(j == 0)
    def _():
        st_ref[...] = jnp.zeros_like(st_ref)

    sub = lax.broadcasted_iota(jnp.int32, (NSUB, LANES), 0)
    rt = lax.broadcasted_iota(jnp.int32, (CHUNK, CHUNK), 0)
    ct = lax.broadcasted_iota(jnp.int32, (CHUNK, CHUNK), 1)
    lmasks = [_same_block(rt, ct, m) for m in LEVELS]
    same_sub = _same_block(rt, ct, SUB)
    dmasks = (same_sub & (rt >= ct), same_sub & (rt <= ct))
    tris = (jnp.where(ct <= rt, 1.0, 0.0).astype(BF16), jnp.where(ct >= rt, 1.0, 0.0).astype(BF16))
    q_refs, z_refs, v_refs = (qf_ref, qb_ref), (zf_ref, zb_ref), (vf_ref, vb_ref)
    acc_refs = (accf_ref, accb_ref)
    base = (j * HBLOCK, (nj - 1 - j) * HBLOCK)

    def step(i, carry):
        r0 = [(pl.multiple_of((HUNROLL * i + u) * CHUNK, CHUNK),
               pl.multiple_of((HSTEPS - 1 - HUNROLL * i - u) * CHUNK, CHUNK)) for u in range(HUNROLL)]
        decays = [[_chunk_decays(r0[u][d], z_refs[d], lb_ref[d:d + 1, :], tris[d]) for d in range(2)]
                  for u in range(HUNROLL)]
        ops = [[_chunk_operands(r0[u][d], d, d == 1, *decays[u][d], q_refs[d], esel_ref, b_ref.at[u], c_ref.at[u],
                                fac_ref.at[u], sub) for d in range(2)] for u in range(HUNROLL)]
        for u in range(HUNROLL):
            parts = [_chunk_state_and_scores(r0[u][d], d, *ops[u][d], v_refs[d], st_ref, fac_ref.at[u], dmasks[d],
                                             same_sub, lmasks) for d in range(2)]
            for d in range(2):
                _chunk_outputs(pl.multiple_of(base[d] + r0[u][d], CHUNK), parts[d], acc_refs[d])
        return carry

    lax.fori_loop(0, HSTEPS // HUNROLL, step, 0)

    @pl.when(j == nj - 1)
    def _():
        for h in range(REC_HEADS):
            cols = slice(h * REC_V_DIM, (h + 1) * REC_V_DIM)
            o = _rms(accf_ref[:, cols] + accb_ref[:, cols], gain_ref[...])
            g = g_ref[:, cols].astype(F32)
            o_ref[:, cols] = (o * (g * jax.nn.sigmoid(g))).astype(o_ref.dtype)


def _hgrn(q, zf, zb, v, g, lb, gain, esel, B, S):
    nj = S // HBLOCK
    fwd = lambda wd: pl.BlockSpec((HBLOCK, wd), lambda b, j: (b * nj + j, 0))
    bwd = lambda wd: pl.BlockSpec((HBLOCK, wd), lambda b, j: (b * nj + nj - 1 - j, 0))
    seq = lambda: pl.BlockSpec((S, REC_WIDTH), lambda b, j: (b, 0))
    return pl.pallas_call(
        functools.partial(_hgrn_body, S=S),
        grid=(B, nj),
        in_specs=[
            fwd(REC_WIDTH), bwd(REC_WIDTH),
            fwd(REC_WIDTH), bwd(REC_WIDTH),
            fwd(REC_WIDTH), bwd(REC_WIDTH),
            seq(),
            _resident((2, REC_WIDTH)),
            _resident((1, REC_V_DIM)),
            _resident((SUB * REC_K_DIM, CHUNK)),
        ],
        out_specs=seq(),
        out_shape=jax.ShapeDtypeStruct((B * S, REC_WIDTH), BF16),
        scratch_shapes=[
            pltpu.VMEM((2, REC_HEADS, REC_V_DIM, REC_K_DIM), F32),
            pltpu.VMEM((HUNROLL, 2, REC_HEADS, CHUNK, REC_K_DIM), F32),
            pltpu.VMEM((HUNROLL, 2, REC_HEADS, CHUNK, REC_K_DIM), F32),
            pltpu.VMEM((HUNROLL, 2, REC_HEADS, NFAC * NSUB, REC_K_DIM), F32),
            pltpu.VMEM((S, REC_WIDTH), F32),
            pltpu.VMEM((S, REC_WIDTH), F32),
        ],
        compiler_params=_params("arbitrary", "arbitrary"),
        name="hgrn2",
    )(q, q, zf, zb, v, v, g, lb, gain, esel)


FFN_TILE = 256


def _merge_ffn_body(x_ref, yp_ref, ya_ref, yr_ref, gate_ref, wup_ref, wua_ref, wur_ref, wo_ref,
                    n2_ref, wfi_ref, wfo_ref, fg_ref, o_ref, *, final):
    merged = None
    for k, (y_ref, w_ref) in enumerate(((yp_ref, wup_ref), (ya_ref, wua_ref), (yr_ref, wur_ref))):
        gate = jax.nn.sigmoid(gate_ref[:, k * D_MODEL:(k + 1) * D_MODEL].astype(F32))
        term = gate * jnp.dot(y_ref[...], w_ref[...], preferred_element_type=F32)
        merged = term if merged is None else merged + term
    x1 = x_ref[...] + jnp.dot(merged.astype(BF16), wo_ref[...], preferred_element_type=F32)
    h2 = _rms(x1, n2_ref[...]).astype(BF16)
    acc = x1
    for c in range(0, FFN_HIDDEN, FFN_TILE):
        gp = jnp.dot(h2, wfi_ref[:, c:c + FFN_TILE], preferred_element_type=F32)
        up = jnp.dot(h2, wfi_ref[:, FFN_HIDDEN + c:FFN_HIDDEN + c + FFN_TILE], preferred_element_type=F32)
        act = (gp * jax.nn.sigmoid(gp) * up).astype(BF16)
        acc = acc + jnp.dot(act, wfo_ref[c:c + FFN_TILE, :], preferred_element_type=F32)
    if final:
        acc = _rms(acc, fg_ref[...])
    o_ref[...] = acc


def _merge_ffn(x2, yp, ya, yr, gate, wup, wua, wur, wo, n2, wfi, wfo, fg, tm, final):
    T = x2.shape[0]
    row = lambda wd: pl.BlockSpec((tm, wd), lambda i: (i, 0))
    return pl.pallas_call(
        functools.partial(_merge_ffn_body, final=final),
        grid=(T // tm,),
        in_specs=[
            row(D_MODEL), row(POOL_WIDTH), row(ATT_WIDTH), row(REC_WIDTH), row(N_BRANCHES * D_MODEL),
            _resident((POOL_WIDTH, D_MODEL)), _resident((ATT_WIDTH, D_MODEL)), _resident((REC_WIDTH, D_MODEL)),
            _resident((D_MODEL, D_MODEL)), _resident((1, D_MODEL)),
            _resident((D_MODEL, 2 * FFN_HIDDEN)), _resident((FFN_HIDDEN, D_MODEL)), _resident((1, D_MODEL)),
        ],
        out_specs=row(D_MODEL),
        out_shape=jax.ShapeDtypeStruct((T, D_MODEL), F32),
        compiler_params=_params("arbitrary"),
        name="merge_ffn",
    )(x2, yp, ya, yr, gate, wup, wua, wur, wo, n2, wfi, wfo, fg)


def _diag_select():
    j = lax.broadcasted_iota(jnp.int32, (SUB * REC_K_DIM, CHUNK), 0) // REC_K_DIM
    c = lax.broadcasted_iota(jnp.int32, (SUB * REC_K_DIM, CHUNK), 1)
    return (c % SUB == j).astype(BF16)


@jax.jit
def kernel(x, norm1_g, w_in, pool_w, pool_scale, lam_q1, lam_k1, lam_q2, lam_k2, diff_norm_g, hgrn_lb,
           hgrn_norm_g, w_up_pool, w_up_attn, w_up_rec, w_out, norm2_g, w_ffn_in, w_ffn_out, final_norm_g):
    B, S, D = x.shape
    assert D == D_MODEL and w_in.shape[-1] == IN_COLS and S % HBLOCK == 0
    T = B * S
    tm = min(512, T)
    qb = min(256, S)
    x2 = x.reshape(T, D).astype(F32)

    lb_all = jnp.cumsum(jax.nn.softmax(hgrn_lb.astype(F32), axis=1), axis=1)
    lb_all = lb_all - lb_all[:, :1]
    esel = _diag_select()
    row = lambda a: a.reshape(1, -1).astype(F32)

    for l in range(DEPTH):
        lambda_init = 0.8 - 0.6 * math.exp(-0.3 * l)
        (u_pool, aq, ak, av, rq, rf, rb, ri, rg, gate) = _inproj(x2, row(norm1_g[l]), w_in[l].astype(BF16), tm)
        y_pool = _pool(u_pool, pool_w[l].astype(BF16), row(pool_scale[l]), B, S)
        y_attn = _attn(aq, ak, av, row(lam_q1[l]), row(lam_k1[l]), row(lam_q2[l]), row(lam_k2[l]),
                       row(diff_norm_g[l]), B, S, qb, lambda_init)
        y_rec = _hgrn(rq, rf, rb, ri, rg, lb_all[:, l], row(hgrn_norm_g[l]), esel, B, S)
        x2 = _merge_ffn(x2, y_pool, y_attn, y_rec, gate,
                        w_up_pool[l].astype(BF16), w_up_attn[l].astype(BF16), w_up_rec[l].astype(BF16),
                        w_out[l].astype(BF16), row(norm2_g[l]), w_ffn_in[l].astype(BF16),
                        w_ffn_out[l].astype(BF16), row(final_norm_g), tm, l == DEPTH - 1)
    return x2.reshape(B, S, D).astype(x.dtype)
```
